```python
import jax, jax.numpy as jnp
from jax import lax
import numpy as np

D_MODEL = 1024
BATCH = 1
SEQ = 16384
DEPTH = 1
DEC_BATCH = 32
DEC_SEQ = 4
PAST_LEN = 16384
PAGE_SIZE = 128

D_CONV = D_MODEL
CONV_W = 3
H_F = 16
HD_F = 64
D_FOX = H_F * HD_F
H_M = 4
HD_M = D_MODEL // H_M
D_MEM = H_M * HD_M
N_MEM = 256
N_BRANCH = 3
Q_BLOCK = 128
RMS_EPS = 1e-6
FORGET_BIAS_LO = 2.0
FORGET_BIAS_HI = 6.0
IN_SIZES = [D_CONV] * 4 + [D_FOX] * 4 + [H_F] + [D_MEM] * 2 + [D_MODEL] * N_BRANCH
N_IN = sum(IN_SIZES)

kernel_name = "hybrid_conv_fox_mem_step"


def _rmsnorm(x, g):
    xf = x.astype(jnp.float32)
    r = lax.rsqrt(jnp.mean(xf * xf, axis=-1, keepdims=True) + RMS_EPS)
    return (xf * r * g.astype(jnp.float32)).astype(x.dtype)


def _project(h, w_in, b_f):
    b, t, _ = h.shape
    offs = [int(o) for o in np.cumsum(IN_SIZES)[:-1]]
    (xc, bc, cc, zc, q, k, v, zf, fl, qm, zm, ga, gf, gm) = jnp.split(h @ w_in, offs, axis=-1)
    q = q.reshape(b, t, H_F, HD_F)
    k = k.reshape(b, t, H_F, HD_F)
    v = v.reshape(b, t, H_F, HD_F)
    qm = qm.reshape(b, t, H_M, HD_M)
    logf = jax.nn.log_sigmoid((fl + b_f).astype(jnp.float32))
    return xc, bc, cc, zc, q, k, v, zf, logf, qm, zm, ga, gf, gm


def _dwconv(u_ctx, w_conv, b_conv):
    rhs = w_conv[:, None, :].astype(u_ctx.dtype)
    y = lax.conv_general_dilated(u_ctx, rhs, window_strides=(1,), padding='VALID',
                                 dimension_numbers=('NWC', 'WIO', 'NWC'),
                                 feature_group_count=u_ctx.shape[-1])
    return y + b_conv


def _fox_prompt(q, k, v, logf):
    b, s, h, d = q.shape
    scale = d ** -0.5
    dcum = jnp.cumsum(logf, axis=1)
    dk = dcum.transpose(0, 2, 1)[:, :, None, :]
    nb = s // Q_BLOCK
    qb = q.reshape(b, nb, Q_BLOCK, h, d).swapaxes(0, 1)
    db = dcum.reshape(b, nb, Q_BLOCK, h).swapaxes(0, 1)
    key_pos = jnp.arange(s)

    def block(args):
        qi, di, i = args
        qpos = i * Q_BLOCK + jnp.arange(Q_BLOCK)
        logits = jnp.einsum('bqhd,bshd->bhqs', qi, k, preferred_element_type=jnp.float32) * scale
        logits = logits + di.transpose(0, 2, 1)[..., None] - dk
        logits = jnp.where(key_pos[None, :] <= qpos[:, None], logits, -jnp.inf)
        p = jax.nn.softmax(logits, axis=-1)
        return jnp.einsum('bhqs,bshd->bqhd', p.astype(v.dtype), v)

    out = lax.map(block, (qb, db, jnp.arange(nb)))
    return out.swapaxes(0, 1).reshape(b, s, h, d)


def _fox_sample(q, k_new, v_new, logf_new, k_past, v_past, logf_past):
    b, t, h, d = q.shape
    n_past = k_past.shape[1]
    scale = d ** -0.5
    dpast = jnp.cumsum(logf_past, axis=1)
    dnew = dpast[:, -1:, :] + jnp.cumsum(logf_new, axis=1)
    dq = dnew.transpose(0, 2, 1)[..., None]
    s_past = jnp.einsum('bthd,bshd->bhts', q, k_past, preferred_element_type=jnp.float32) * scale
    s_past = s_past + dq - dpast.transpose(0, 2, 1)[:, :, None, :]
    s_new = jnp.einsum('bthd,bshd->bhts', q, k_new, preferred_element_type=jnp.float32) * scale
    s_new = s_new + dq - dnew.transpose(0, 2, 1)[:, :, None, :]
    s_new = jnp.where(jnp.tril(jnp.ones((t, t), dtype=bool)), s_new, -jnp.inf)
    p = jax.nn.softmax(jnp.concatenate([s_past, s_new], axis=-1), axis=-1)
    out = (jnp.einsum('bhts,bshd->bthd', p[..., :n_past].astype(v_past.dtype), v_past)
           + jnp.einsum('bhts,bshd->bthd', p[..., n_past:].astype(v_new.dtype), v_new))
    return out


def _mem_attn(q, k, v):
    scale = q.shape[-1] ** -0.5
    logits = jnp.einsum('bthd,bmhd->bhtm', q, k, preferred_element_type=jnp.float32) * scale
    p = jax.nn.softmax(logits, axis=-1)
    return jnp.einsum('bhtm,bmhd->bthd', p.astype(v.dtype), v)


def _merge_out(x, o_c, o_f, o_m, ga, gf, gm, w_up_conv, w_up_fox, w_up_mem, w_o, g_post):
    m = (jax.nn.sigmoid(ga) * (o_c @ w_up_conv)
         + jax.nn.sigmoid(gf) * (o_f @ w_up_fox)
         + jax.nn.sigmoid(gm) * (o_m @ w_up_mem))
    return x + _rmsnorm(m @ w_o, g_post)


def setup_inputs(seed: int = 0) -> dict:
    key = jax.random.key(seed)
    ks = jax.random.split(key, 24)
    n_pages = PAST_LEN // PAGE_SIZE
    n_used = DEC_BATCH * n_pages
    n_phys = n_used + max(1, n_used // 4)

    def nrm(k, shape, s=1.0):
        return s * jax.random.normal(k, shape, jnp.float32)

    x_prompt = nrm(ks[0], (BATCH, SEQ, D_MODEL))
    x_sample = nrm(ks[1], (DEC_BATCH, DEC_SEQ, D_MODEL))
    cache_fox_k = nrm(ks[2], (n_phys, PAGE_SIZE, H_F, HD_F))
    cache_fox_v = nrm(ks[3], (n_phys, PAGE_SIZE, H_F, HD_F))
    cache_fox_logf = -jax.nn.softplus(-jax.random.uniform(
        ks[4], (n_phys, PAGE_SIZE, H_F), jnp.float32, FORGET_BIAS_LO, FORGET_BIAS_HI))
    state_conv = nrm(ks[5], (DEC_BATCH, CONV_W - 1, D_CONV))
    cache_mem_k = nrm(ks[6], (DEC_BATCH, N_MEM, H_M, HD_M))
    cache_mem_v = nrm(ks[7], (DEC_BATCH, N_MEM, H_M, HD_M))
    page_table = jax.random.permutation(ks[8], n_phys)[:n_used].reshape(DEC_BATCH, n_pages).astype(jnp.int32)
    mem_prompt = nrm(ks[9], (BATCH, N_MEM, D_MODEL))
    g_pre = 1.0 + nrm(ks[10], (D_MODEL,), 0.05)
    w_in = nrm(ks[11], (D_MODEL, N_IN), D_MODEL ** -0.5)
    b_f = jax.random.uniform(ks[12], (H_F,), jnp.float32, FORGET_BIAS_LO, FORGET_BIAS_HI)
    w_conv = nrm(ks[13], (CONV_W, D_CONV), CONV_W ** -0.5)
    b_conv = nrm(ks[14], (D_CONV,), 0.02)
    g_mem = 1.0 + nrm(ks[15], (D_MODEL,), 0.05)
    w_mem_kv = nrm(ks[16], (D_MODEL, 2 * D_MEM), D_MODEL ** -0.5)
    w_up_conv = nrm(ks[17], (D_CONV, D_MODEL), D_CONV ** -0.5)
    w_up_fox = nrm(ks[18], (D_FOX, D_MODEL), D_FOX ** -0.5)
    w_up_mem = nrm(ks[19], (D_MEM, D_MODEL), D_MEM ** -0.5)
    w_o = nrm(ks[20], (D_MODEL, D_MODEL), D_MODEL ** -0.5)
    g_post = 1.0 + nrm(ks[21], (D_MODEL,), 0.05)
    return {"x_prompt": x_prompt, "x_sample": x_sample,
            "cache_fox_k": cache_fox_k, "cache_fox_v": cache_fox_v, "cache_fox_logf": cache_fox_logf,
            "state_conv": state_conv, "cache_mem_k": cache_mem_k, "cache_mem_v": cache_mem_v,
            "page_table": page_table, "mem_prompt": mem_prompt,
            "g_pre": g_pre, "w_in": w_in, "b_f": b_f, "w_conv": w_conv, "b_conv": b_conv,
            "g_mem": g_mem, "w_mem_kv": w_mem_kv, "w_up_conv": w_up_conv, "w_up_fox": w_up_fox,
            "w_up_mem": w_up_mem, "w_o": w_o, "g_post": g_post}


def reference(x_prompt, x_sample, cache_fox_k, cache_fox_v, cache_fox_logf, state_conv,
              cache_mem_k, cache_mem_v, page_table, mem_prompt,
              g_pre, w_in, b_f, w_conv, b_conv, g_mem, w_mem_kv,
              w_up_conv, w_up_fox, w_up_mem, w_o, g_post):
    xp = x_prompt
    b, s, _ = xp.shape
    for _layer in range(DEPTH):
        h = _rmsnorm(xp, g_pre)
        xc, bc, cc, zc, q, k, v, zf, logf, qm, zm, ga, gf, gm = _project(h, w_in, b_f)
        u = cc * xc
        u_ctx = jnp.pad(u, ((0, 0), (CONV_W - 1, 0), (0, 0)))
        o_c = bc * _dwconv(u_ctx, w_conv, b_conv) * jax.nn.silu(zc)
        o_f = _fox_prompt(q, k, v, logf).reshape(b, s, D_FOX) * jax.nn.silu(zf)
        mem_n = _rmsnorm(mem_prompt, g_mem)
        mk, mv = jnp.split(mem_n @ w_mem_kv, 2, axis=-1)
        mk = mk.reshape(b, N_MEM, H_M, HD_M)
        mv = mv.reshape(b, N_MEM, H_M, HD_M)
        o_m = _mem_attn(qm, mk, mv).reshape(b, s, D_MEM) * jax.nn.silu(zm)
        xp = _merge_out(xp, o_c, o_f, o_m, ga, gf, gm, w_up_conv, w_up_fox, w_up_mem, w_o, g_post)
        k_prompt, v_prompt = k, v
        logf_prompt = logf.astype(cache_fox_logf.dtype)
        conv_state_prompt = u[:, -(CONV_W - 1):, :]
        mem_k_prompt, mem_v_prompt = mk, mv
    y_prompt = xp

    xs = x_sample
    db_, t, _ = xs.shape
    n_past = page_table.shape[1] * cache_fox_k.shape[1]
    for _layer in range(DEPTH):
        h = _rmsnorm(xs, g_pre)
        xc, bc, cc, zc, q, k, v, zf, logf, qm, zm, ga, gf, gm = _project(h, w_in, b_f)
        u = cc * xc
        u_ctx = jnp.concatenate([state_conv.astype(u.dtype), u], axis=1)
        o_c = bc * _dwconv(u_ctx, w_conv, b_conv) * jax.nn.silu(zc)
        k_past = cache_fox_k[page_table].reshape(db_, n_past, H_F, HD_F)
        v_past = cache_fox_v[page_table].reshape(db_, n_past, H_F, HD_F)
        logf_past = cache_fox_logf[page_table].reshape(db_, n_past, H_F).astype(jnp.float32)
        o_f = _fox_sample(q, k, v, logf, k_past, v_past, logf_past).reshape(db_, t, D_FOX) * jax.nn.silu(zf)
        o_m = _mem_attn(qm, cache_mem_k, cache_mem_v).reshape(db_, t, D_MEM) * jax.nn.silu(zm)
        xs = _merge_out(xs, o_c, o_f, o_m, ga, gf, gm, w_up_conv, w_up_fox, w_up_mem, w_o, g_post)
        k_sample, v_sample = k, v
        logf_sample = logf.astype(cache_fox_logf.dtype)
        conv_state_sample = u_ctx[:, -(CONV_W - 1):, :]
    y_sample = xs

    return (y_prompt, y_sample, k_prompt, v_prompt, logf_prompt, conv_state_prompt,
            mem_k_prompt, mem_v_prompt, k_sample, v_sample, logf_sample, conv_state_sample)
```

```python
import functools

import numpy as np
import jax
import jax.numpy as jnp
from jax import lax
from jax.experimental import pallas as pl
from jax.experimental.pallas import tpu as pltpu

D_MODEL = 1024
H_F = 16
HD_F = 64
H_M = 4
HD_M = 256
N_MEM = 256
PAGE = 128
RMS_EPS = 1e-6
LOG2E = 1.4426950408889634
FOX_QSCALE = (HD_F ** -0.5) * LOG2E
MEM_SCALE = HD_M ** -0.5

F32 = jnp.float32
BF16 = jnp.bfloat16

TM = 256
BLK = 512
PAGES_PER_STEP = 8
SLOT = 128

VMEM_LIMIT = 56 * 1024 * 1024

_NT = (((1,), (1,)), ((), ()))


def _dot(a, b):
    return jnp.dot(a, b, preferred_element_type=F32)


def _dot_nt(a, b):
    return lax.dot_general(a, b, _NT, preferred_element_type=F32)


def _rms(x, g):
    ms = jnp.mean(x * x, axis=-1, keepdims=True)
    return x * lax.rsqrt(ms + RMS_EPS) * g


def _sigmoid(x):
    return 1.0 / (1.0 + jnp.exp(-x))


def _silu(x):
    return x * _sigmoid(x)


def _log_sigmoid(x):
    return -(jnp.maximum(-x, 0.0) + jnp.log1p(jnp.exp(-jnp.abs(x))))


def _split3(x):
    hi = x.astype(BF16)
    r = x - hi.astype(F32)
    mid = r.astype(BF16)
    lo = (r - mid.astype(F32)).astype(BF16)
    return hi, mid, lo


def _lower_tri(n):
    row = lax.broadcasted_iota(jnp.int32, (n, n), 0)
    col = lax.broadcasted_iota(jnp.int32, (n, n), 1)
    return (col <= row).astype(BF16)


def _cumsum_rows(x):
    tri = _lower_tri(x.shape[0])
    a, b, c = _split3(x)
    return _dot(tri, a) + _dot(tri, b) + _dot(tri, c)


def _const_spec(shape):
    zeros = (0,) * len(shape)
    return pl.BlockSpec(shape, lambda *_: zeros, pipeline_mode=pl.Buffered(1))


def _memkv_kernel(mem_ref, g_ref, w_ref, kv_ref, kvb_ref):
    n = _rms(mem_ref[...], g_ref[...]).astype(BF16)
    kv = _dot(n, w_ref[...])
    kv_ref[...] = kv
    kvb_ref[...] = kv.astype(BF16)


def _mem_kv(mem, g_mem, w_kv_b):
    return pl.pallas_call(
        _memkv_kernel,
        out_shape=(jax.ShapeDtypeStruct((N_MEM, 2 * D_MODEL), F32),
                   jax.ShapeDtypeStruct((N_MEM, 2 * D_MODEL), BF16)),
        compiler_params=pltpu.CompilerParams(vmem_limit_bytes=VMEM_LIMIT),
        name="mem_kv",
    )(mem, g_mem, w_kv_b)


def _mem_attention(qm, mk_ref, mv_ref):
    parts = []
    for hh in range(H_M):
        sl = slice(hh * HD_M, (hh + 1) * HD_M)
        s = _dot_nt(qm[:, sl].astype(BF16), mk_ref[:, sl]) * MEM_SCALE
        s = s - jnp.max(s, axis=-1, keepdims=True)
        p = jnp.exp(s)
        p = p / jnp.sum(p, axis=-1, keepdims=True)
        parts.append(_dot(p.astype(BF16), mv_ref[:, sl]))
    return jnp.concatenate(parts, axis=1)


def _branch_cm_kernel(x_ref, g_ref, w_ref, wconv_ref, bconv_ref, mk_ref, mv_ref,
                      wupc_ref, wupm_ref, mcm_ref, cstate_ref, ubuf):
    i = pl.program_id(0)

    @pl.when(i == 0)
    def _():
        ubuf[0:8, :] = jnp.zeros((8, D_MODEL), F32)

    h = _rms(x_ref[...], g_ref[...]).astype(BF16)

    def seg(k):
        return _dot(h, w_ref[:, k * D_MODEL:(k + 1) * D_MODEL])

    u = seg(2) * seg(0)
    ubuf[8:8 + TM, :] = u
    u1 = ubuf[7:7 + TM, :]
    u2 = ubuf[6:6 + TM, :]
    wc = wconv_ref[...]
    conv = wc[0:1, :] * u2 + wc[1:2, :] * u1 + wc[2:3, :] * u + bconv_ref[...]
    cstate_ref[...] = ubuf[TM + 6:TM + 8, :]
    ubuf[0:8, :] = ubuf[TM:TM + 8, :]
    o_c = seg(1) * conv * _silu(seg(3))
    m = _sigmoid(seg(6)) * _dot(o_c.astype(BF16), wupc_ref[...])

    o_m = _mem_attention(seg(4), mk_ref, mv_ref) * _silu(seg(5))
    m = m + _sigmoid(seg(7)) * _dot(o_m.astype(BF16), wupm_ref[...])
    mcm_ref[...] = m


def _branch_cm(x, g_pre, w_cm, w_conv, b_conv, mk_b, mv_b, wupc, wupm):
    s = x.shape[0]
    row = lambda i: (i, 0)
    return pl.pallas_call(
        _branch_cm_kernel,
        grid=(s // TM,),
        in_specs=[
            pl.BlockSpec((TM, D_MODEL), row),
            _const_spec((1, D_MODEL)),
            _const_spec((D_MODEL, 8 * D_MODEL)),
            _const_spec((3, D_MODEL)),
            _const_spec((1, D_MODEL)),
            _const_spec((N_MEM, D_MODEL)),
            _const_spec((N_MEM, D_MODEL)),
            _const_spec((D_MODEL, D_MODEL)),
            _const_spec((D_MODEL, D_MODEL)),
        ],
        out_specs=(pl.BlockSpec((TM, D_MODEL), row),
                   pl.BlockSpec((2, D_MODEL), lambda i: (0, 0))),
        out_shape=(jax.ShapeDtypeStruct((s, D_MODEL), F32),
                   jax.ShapeDtypeStruct((2, D_MODEL), F32)),
        scratch_shapes=[pltpu.VMEM((TM + 8, D_MODEL), F32)],
        compiler_params=pltpu.CompilerParams(
            dimension_semantics=("arbitrary",), vmem_limit_bytes=VMEM_LIMIT),
        name="prompt_branch_cm",
    )(x, g_pre, w_cm, w_conv, b_conv, mk_b, mv_b, wupc, wupm)


def _decay_placement():
    eq = np.zeros((3, SLOT, H_F * SLOT), np.float32)
    ek = np.zeros((3, SLOT, H_F * SLOT), np.float32)
    for h in range(H_F):
        base = h * SLOT + HD_F
        for part in range(3):
            eq[part, h, base + part] = 1.0
            ek[part, h, base + 3 + part] = 1.0
        eq[0, H_F, base + 3:base + 6] = 1.0
        ek[0, H_F, base:base + 3] = 1.0
    return jnp.asarray(eq, BF16), jnp.asarray(ek, BF16)


def _proj_fox_kernel(x_ref, g_ref, wqp_ref, wk_ref, wkp_ref, wv_ref, wvt_ref,
                     wzf_ref, wgf_ref, wf_ref, bf_ref, eq_ref, ek_ref,
                     k_ref, v_ref, qx_ref, kx_ref, vt_ref, zf_ref, sgf_ref,
                     logf_ref, carry):
    i = pl.program_id(0)

    @pl.when(i == 0)
    def _():
        carry[...] = jnp.zeros((1, SLOT), F32)

    h = _rms(x_ref[...], g_ref[...]).astype(BF16)
    k_ref[...] = _dot(h, wk_ref[...])
    v_ref[...] = _dot(h, wv_ref[...])
    vt_ref[...] = _dot_nt(wvt_ref[...], h).astype(BF16)
    zf_ref[...] = _dot(h, wzf_ref[...]).astype(BF16)
    sgf_ref[...] = _sigmoid(_dot(h, wgf_ref[...])).astype(BF16)

    lf = _log_sigmoid(_dot(h, wf_ref[...]) + bf_ref[...])
    logf_ref[...] = lf[:, 0:H_F]
    dcum = _cumsum_rows(lf) + carry[...]
    carry[...] = dcum[TM - 1:TM, :]
    dh, dm, dl = _split3(dcum * LOG2E)
    one_lane = lax.broadcasted_iota(jnp.int32, (TM, SLOT), 1) == H_F
    one = jnp.ones((TM, SLOT), BF16)

    q = _dot(h, wqp_ref[...]) * FOX_QSCALE
    q = (q + _dot(jnp.where(one_lane, one, dh), eq_ref[0])
         + _dot(dm, eq_ref[1]) + _dot(dl, eq_ref[2]))
    kp = _dot(h, wkp_ref[...])
    kp = (kp + _dot(jnp.where(one_lane, one, -dh), ek_ref[0])
          + _dot(-dm, ek_ref[1]) + _dot(-dl, ek_ref[2]))
    for hh in range(H_F):
        sl = slice(hh * SLOT, (hh + 1) * SLOT)
        qx_ref[hh] = q[:, sl].astype(BF16)
        kx_ref[hh] = kp[:, sl].astype(BF16)


def _proj_fox(x, g_pre, wqp, wk, wkp, wv, wvt, wzf, wgf, wf, bf, eq, ek):
    s = x.shape[0]
    row = lambda i: (i, 0)
    slot = lambda i: (0, i, 0)
    return pl.pallas_call(
        _proj_fox_kernel,
        grid=(s // TM,),
        in_specs=[
            pl.BlockSpec((TM, D_MODEL), row),
            _const_spec((1, D_MODEL)),
            _const_spec((D_MODEL, H_F * SLOT)),
            _const_spec((D_MODEL, D_MODEL)),
            _const_spec((D_MODEL, H_F * SLOT)),
            _const_spec((D_MODEL, D_MODEL)),
            _const_spec((D_MODEL, D_MODEL)),
            _const_spec((D_MODEL, D_MODEL)),
            _const_spec((D_MODEL, D_MODEL)),
            _const_spec((D_MODEL, SLOT)),
            _const_spec((1, SLOT)),
            _const_spec((3, SLOT, H_F * SLOT)),
            _const_spec((3, SLOT, H_F * SLOT)),
        ],
        out_specs=(
            pl.BlockSpec((TM, D_MODEL), row),
            pl.BlockSpec((TM, D_MODEL), row),
            pl.BlockSpec((H_F, TM, SLOT), slot),
            pl.BlockSpec((H_F, TM, SLOT), slot),
            pl.BlockSpec((D_MODEL, TM), lambda i: (0, i)),
            pl.BlockSpec((TM, D_MODEL), row),
            pl.BlockSpec((TM, D_MODEL), row),
            pl.BlockSpec((TM, H_F), row),
        ),
        out_shape=(
            jax.ShapeDtypeStruct((s, D_MODEL), F32),
            jax.ShapeDtypeStruct((s, D_MODEL), F32),
            jax.ShapeDtypeStruct((H_F, s, SLOT), BF16),
            jax.ShapeDtypeStruct((H_F, s, SLOT), BF16),
            jax.ShapeDtypeStruct((D_MODEL, s), BF16),
            jax.ShapeDtypeStruct((s, D_MODEL), BF16),
            jax.ShapeDtypeStruct((s, D_MODEL), BF16),
            jax.ShapeDtypeStruct((s, H_F), F32),
        ),
        scratch_shapes=[pltpu.VMEM((1, SLOT), F32)],
        compiler_params=pltpu.CompilerParams(
            dimension_semantics=("arbitrary",), vmem_limit_bytes=VMEM_LIMIT),
        name="prompt_proj_fox",
    )(x, g_pre, wqp, wk, wkp, wv, wvt, wzf, wgf, wf, bf, eq, ek)


def _attn_kernel(it_ref, jt_ref, qx_ref, kx_ref, vt_ref, o_ref, m_sc, l_sc, acc_sc):
    step = pl.program_id(0)
    i = it_ref[step]
    j = jt_ref[step]

    @pl.when(j == 0)
    def _():
        m_sc[...] = jnp.full((H_F, BLK), -jnp.inf, F32)
        l_sc[...] = jnp.zeros((H_F, BLK), F32)
        acc_sc[...] = jnp.zeros((D_MODEL, BLK), F32)

    def block(masked):
        if masked:
            key = lax.broadcasted_iota(jnp.int32, (BLK, BLK), 0)
            qry = lax.broadcasted_iota(jnp.int32, (BLK, BLK), 1)
            keep = key <= qry
        for hh in range(H_F):
            st = _dot_nt(kx_ref[hh], qx_ref[hh])
            if masked:
                st = jnp.where(keep, st, -jnp.inf)
            m_old = m_sc[hh:hh + 1, :]
            m_new = jnp.maximum(m_old, jnp.max(st, axis=0, keepdims=True))
            alpha = jnp.exp2(m_old - m_new)
            p = jnp.exp2(st - m_new)
            l_sc[hh:hh + 1, :] = alpha * l_sc[hh:hh + 1, :] + jnp.sum(p, axis=0, keepdims=True)
            m_sc[hh:hh + 1, :] = m_new
            rows = slice(hh * HD_F, (hh + 1) * HD_F)
            pv = _dot(vt_ref[rows, :], p.astype(BF16))
            acc_sc[rows, :] = alpha * acc_sc[rows, :] + pv

    @pl.when(j < i)
    def _():
        block(False)

    @pl.when(j == i)
    def _():
        block(True)
        for hh in range(H_F):
            rows = slice(hh * HD_F, (hh + 1) * HD_F)
            acc_sc[rows, :] = acc_sc[rows, :] * (1.0 / l_sc[hh:hh + 1, :])
        o_ref[...] = acc_sc[...].T.astype(BF16)


def _attn_prompt(qx, kx, vt):
    s = qx.shape[1]
    nb = s // BLK
    it = np.concatenate([np.full(i + 1, i) for i in range(nb)]).astype(np.int32)
    jt = np.concatenate([np.arange(i + 1) for i in range(nb)]).astype(np.int32)
    grid_spec = pltpu.PrefetchScalarGridSpec(
        num_scalar_prefetch=2,
        grid=(it.shape[0],),
        in_specs=[
            pl.BlockSpec((H_F, BLK, SLOT), lambda t, it, jt: (0, it[t], 0)),
            pl.BlockSpec((H_F, BLK, SLOT), lambda t, it, jt: (0, jt[t], 0)),
            pl.BlockSpec((D_MODEL, BLK), lambda t, it, jt: (0, jt[t])),
        ],
        out_specs=pl.BlockSpec((BLK, D_MODEL), lambda t, it, jt: (it[t], 0)),
        scratch_shapes=[pltpu.VMEM((H_F, BLK), F32),
                        pltpu.VMEM((H_F, BLK), F32),
                        pltpu.VMEM((D_MODEL, BLK), F32)],
    )
    return pl.pallas_call(
        _attn_kernel,
        grid_spec=grid_spec,
        out_shape=jax.ShapeDtypeStruct((s, D_MODEL), BF16),
        compiler_params=pltpu.CompilerParams(
            dimension_semantics=("arbitrary",), vmem_limit_bytes=VMEM_LIMIT),
        name="prompt_fox_attention",
    )(jnp.asarray(it), jnp.asarray(jt), qx, kx, vt)


def _final_kernel(x_ref, mcm_ref, of_ref, zf_ref, sgf_ref, wupf_ref, wo_ref, g_ref, y_ref):
    og = (of_ref[...].astype(F32) * _silu(zf_ref[...].astype(F32))).astype(BF16)
    m = mcm_ref[...] + sgf_ref[...].astype(F32) * _dot(og, wupf_ref[...])
    y_ref[...] = x_ref[...] + _rms(_dot(m.astype(BF16), wo_ref[...]), g_ref[...])


def _final_prompt(x, mcm, of, zf, sgf, wupf, wo, g_post):
    s = x.shape[0]
    row = lambda i: (i, 0)
    tile = pl.BlockSpec((TM, D_MODEL), row)
    return pl.pallas_call(
        _final_kernel,
        grid=(s // TM,),
        in_specs=[tile, tile, tile, tile, tile,
                  _const_spec((D_MODEL, D_MODEL)),
                  _const_spec((D_MODEL, D_MODEL)),
                  _const_spec((1, D_MODEL))],
        out_specs=tile,
        out_shape=jax.ShapeDtypeStruct((s, D_MODEL), F32),
        compiler_params=pltpu.CompilerParams(
            dimension_semantics=("arbitrary",), vmem_limit_bytes=VMEM_LIMIT),
        name="prompt_merge_out",
    )(x, mcm, of, zf, sgf, wupf, wo, g_post)


N_SEG = 14
SEG_FORGET = 13


def _proj_sample_kernel(x_ref, g_ref, w_ref, bf_ref, p_ref):
    j = pl.program_id(0)
    h = _rms(x_ref[...], g_ref[...]).astype(BF16)
    acc = _dot(h, w_ref[...])

    @pl.when(j != SEG_FORGET)
    def _():
        p_ref[...] = acc

    @pl.when(j == SEG_FORGET)
    def _():
        p_ref[...] = _log_sigmoid(acc + bf_ref[...])


def _proj_sample(x, g_pre, w_all, bf_pad):
    n = x.shape[0]
    return pl.pallas_call(
        _proj_sample_kernel,
        grid=(N_SEG,),
        in_specs=[pl.BlockSpec((n, D_MODEL), lambda j: (0, 0)),
                  pl.BlockSpec((1, D_MODEL), lambda j: (0, 0)),
                  pl.BlockSpec((D_MODEL, D_MODEL), lambda j: (0, j)),
                  pl.BlockSpec((1, D_MODEL), lambda j: (0, 0))],
        out_specs=pl.BlockSpec((n, D_MODEL), lambda j: (0, j)),
        out_shape=jax.ShapeDtypeStruct((n, N_SEG * D_MODEL), F32),
        compiler_params=pltpu.CompilerParams(
            dimension_semantics=("arbitrary",), vmem_limit_bytes=VMEM_LIMIT),
        name="sample_proj",
    )(x, g_pre, w_all, bf_pad)


T8 = 8
ROWS_F = H_F * T8


def _attn_sample_kernel(pt_ref, q_ref, kn_ref, vn_ref, lfn_ref, *refs, n_new):
    pp = PAGES_PER_STEP
    k_refs = refs[0:pp]
    v_refs = refs[pp:2 * pp]
    lf_refs = refs[2 * pp:3 * pp]
    o_ref = refs[3 * pp]
    qbd, kbuf, vbuf, bias, lfbuf, m_sc, l_sc, acc_sc, carry = refs[3 * pp + 1:]
    c = pl.program_id(1)
    lane_head = lax.broadcasted_iota(jnp.int32, (T8, D_MODEL), 1) // HD_F

    @pl.when(c == 0)
    def _():
        q8 = q_ref[0] * FOX_QSCALE
        qbd[...] = jnp.concatenate(
            [jnp.where(lane_head == hh, q8, 0.0) for hh in range(H_F)], axis=0).astype(BF16)
        m_sc[...] = jnp.full((ROWS_F, 1), -jnp.inf, F32)
        l_sc[...] = jnp.zeros((ROWS_F, 1), F32)
        acc_sc[...] = jnp.zeros((ROWS_F, D_MODEL), F32)
        carry[...] = jnp.zeros((1, SLOT), F32)
        lfbuf[...] = jnp.zeros((PAGE, SLOT), F32)

    expand = (lax.broadcasted_iota(jnp.int32, (ROWS_F, SLOT), 0) // T8
              == lax.broadcasted_iota(jnp.int32, (ROWS_F, SLOT), 1)).astype(BF16)

    def decay_bias(lf_page):
        lfbuf[:, 0:H_F] = lf_page
        dcum = _cumsum_rows(lfbuf[...]) + carry[...]
        carry[...] = dcum[PAGE - 1:PAGE, :]
        a, b, d = _split3(-(dcum * LOG2E))
        return _dot_nt(expand, a) + _dot_nt(expand, b) + _dot_nt(expand, d)

    def update(s, vals):
        m_old = m_sc[...]
        m_new = jnp.maximum(m_old, jnp.max(s, axis=1, keepdims=True))
        alpha = jnp.exp2(m_old - m_new)
        p = jnp.exp2(s - m_new)
        l_sc[...] = alpha * l_sc[...] + jnp.sum(p, axis=1, keepdims=True)
        acc_sc[...] = alpha * acc_sc[...] + _dot(p.astype(BF16), vals)
        m_sc[...] = m_new

    for r in range(pp):
        rows = slice(r * PAGE, (r + 1) * PAGE)
        kbuf[rows, :] = k_refs[r][0].astype(BF16)
        vbuf[rows, :] = v_refs[r][0].astype(BF16)
        bias[:, rows] = decay_bias(lf_refs[r][0])
    update(_dot_nt(qbd[...], kbuf[...]) + bias[...], vbuf[...])

    @pl.when(c == pl.num_programs(1) - 1)
    def _():
        pad = jnp.zeros((PAGE - T8, D_MODEL), F32)
        kn = jnp.concatenate([kn_ref[0], pad], axis=0).astype(BF16)
        vn = jnp.concatenate([vn_ref[0], pad], axis=0).astype(BF16)
        lfbuf[...] = jnp.zeros((PAGE, SLOT), F32)
        b_new = decay_bias(jnp.concatenate(
            [lfn_ref[0], jnp.zeros((PAGE - T8, H_F), F32)], axis=0))
        s = _dot_nt(qbd[...], kn) + b_new
        tok = lax.broadcasted_iota(jnp.int32, (ROWS_F, PAGE), 1)
        t_row = lax.broadcasted_iota(jnp.int32, (ROWS_F, PAGE), 0) % T8
        s = jnp.where((tok < n_new) & (tok <= t_row), s, -jnp.inf)
        update(s, vn)
        out = acc_sc[...] * (1.0 / l_sc[...])
        o8 = jnp.zeros((T8, D_MODEL), F32)
        for hh in range(H_F):
            o8 = o8 + jnp.where(lane_head == hh, out[hh * T8:(hh + 1) * T8, :], 0.0)
        o_ref[0] = o8


def _attn_sample(page_table, q8, kn8, vn8, lfn8, cache_k, cache_v, cache_lf, n_new):
    nb, n_pages = page_table.shape
    pp = PAGES_PER_STEP
    per_b = lambda b, c, pt: (b, 0, 0)

    def page_map(r):
        return lambda b, c, pt: (pt[b * n_pages + c * pp + r], 0, 0)

    in_specs = [pl.BlockSpec((1, T8, D_MODEL), per_b),
                pl.BlockSpec((1, T8, D_MODEL), per_b),
                pl.BlockSpec((1, T8, D_MODEL), per_b),
                pl.BlockSpec((1, T8, H_F), per_b)]
    in_specs += [pl.BlockSpec((1, PAGE, D_MODEL), page_map(r)) for r in range(pp)]
    in_specs += [pl.BlockSpec((1, PAGE, D_MODEL), page_map(r)) for r in range(pp)]
    in_specs += [pl.BlockSpec((1, PAGE, H_F), page_map(r)) for r in range(pp)]
    grid_spec = pltpu.PrefetchScalarGridSpec(
        num_scalar_prefetch=1,
        grid=(nb, n_pages // pp),
        in_specs=in_specs,
        out_specs=pl.BlockSpec((1, T8, D_MODEL), per_b),
        scratch_shapes=[
            pltpu.VMEM((ROWS_F, D_MODEL), BF16),
            pltpu.VMEM((pp * PAGE, D_MODEL), BF16),
            pltpu.VMEM((pp * PAGE, D_MODEL), BF16),
            pltpu.VMEM((ROWS_F, pp * PAGE), F32),
            pltpu.VMEM((PAGE, SLOT), F32),
            pltpu.VMEM((ROWS_F, 1), F32),
            pltpu.VMEM((ROWS_F, 1), F32),
            pltpu.VMEM((ROWS_F, D_MODEL), F32),
            pltpu.VMEM((1, SLOT), F32),
        ],
    )
    args = [page_table.reshape(-1), q8, kn8, vn8, lfn8]
    args += [cache_k] * pp + [cache_v] * pp + [cache_lf] * pp
    return pl.pallas_call(
        functools.partial(_attn_sample_kernel, n_new=n_new),
        grid_spec=grid_spec,
        out_shape=jax.ShapeDtypeStruct((nb, T8, D_MODEL), F32),
        compiler_params=pltpu.CompilerParams(
            dimension_semantics=("arbitrary", "arbitrary"), vmem_limit_bytes=VMEM_LIMIT),
        name="sample_fox_attention",
    )(*args)


ROWS_M = H_M * T8


def _mem_sample_kernel(q_ref, k_ref, v_ref, o_ref):
    lane_head = lax.broadcasted_iota(jnp.int32, (T8, D_MODEL), 1) // HD_M
    q8 = q_ref[0]
    qbd = jnp.concatenate(
        [jnp.where(lane_head == hh, q8, 0.0) for hh in range(H_M)], axis=0).astype(BF16)
    s = _dot_nt(qbd, k_ref[0].astype(BF16)) * MEM_SCALE
    s = s - jnp.max(s, axis=-1, keepdims=True)
    p = jnp.exp(s)
    p = p / jnp.sum(p, axis=-1, keepdims=True)
    out = _dot(p.astype(BF16), v_ref[0].astype(BF16))
    o8 = jnp.zeros((T8, D_MODEL), F32)
    for hh in range(H_M):
        o8 = o8 + jnp.where(lane_head == hh, out[hh * T8:(hh + 1) * T8, :], 0.0)
    o_ref[0] = o8


def _mem_sample(qm8, mem_k, mem_v):
    nb = qm8.shape[0]
    per_b = lambda b: (b, 0, 0)
    return pl.pallas_call(
        _mem_sample_kernel,
        grid=(nb,),
        in_specs=[pl.BlockSpec((1, T8, D_MODEL), per_b),
                  pl.BlockSpec((1, N_MEM, D_MODEL), per_b),
                  pl.BlockSpec((1, N_MEM, D_MODEL), per_b)],
        out_specs=pl.BlockSpec((1, T8, D_MODEL), per_b),
        out_shape=jax.ShapeDtypeStruct((nb, T8, D_MODEL), F32),
        compiler_params=pltpu.CompilerParams(
            dimension_semantics=("arbitrary",), vmem_limit_bytes=VMEM_LIMIT),
        name="sample_mem_attention",
    )(qm8, mem_k, mem_v)


def _merge_sample_kernel(x_ref, p_ref, st0_ref, st1_ref, of_ref, om_ref, wconv_ref, bconv_ref,
                         wupc_ref, wupf_ref, wupm_ref, wo_ref, g_ref, y_ref, u_ref, ubuf,
                         *, n_new):
    n = x_ref.shape[0]

    def seg(k):
        return p_ref[:, k * D_MODEL:(k + 1) * D_MODEL]

    u = seg(2) * seg(0)
    u_ref[...] = u
    ubuf[0:8, :] = jnp.zeros((8, D_MODEL), F32)
    ubuf[8:8 + n, :] = u
    t = lax.broadcasted_iota(jnp.int32, (n, D_MODEL), 0) % n_new
    u1 = jnp.where(t == 0, st1_ref[...], ubuf[7:7 + n, :])
    u2 = jnp.where(t == 0, st0_ref[...], jnp.where(t == 1, st1_ref[...], ubuf[6:6 + n, :]))
    wc = wconv_ref[...]
    conv = wc[0:1, :] * u2 + wc[1:2, :] * u1 + wc[2:3, :] * u + bconv_ref[...]
    o_c = seg(1) * conv * _silu(seg(3))
    o_f = of_ref[...] * _silu(seg(7))
    o_m = om_ref[...] * _silu(seg(9))
    m = (_sigmoid(seg(10)) * _dot(o_c.astype(BF16), wupc_ref[...])
         + _sigmoid(seg(11)) * _dot(o_f.astype(BF16), wupf_ref[...])
         + _sigmoid(seg(12)) * _dot(o_m.astype(BF16), wupm_ref[...]))
    y_ref[...] = x_ref[...] + _rms(_dot(m.astype(BF16), wo_ref[...]), g_ref[...])


def _merge_sample(x, ps, st0, st1, of, om, w_conv, b_conv, wupc, wupf, wupm, wo, g_post, n_new):
    n = x.shape[0]
    return pl.pallas_call(
        functools.partial(_merge_sample_kernel, n_new=n_new),
        out_shape=(jax.ShapeDtypeStruct((n, D_MODEL), F32),
                   jax.ShapeDtypeStruct((n, D_MODEL), F32)),
        scratch_shapes=[pltpu.VMEM((n + 8, D_MODEL), F32)],
        compiler_params=pltpu.CompilerParams(vmem_limit_bytes=VMEM_LIMIT),
        name="sample_merge_out",
    )(x, ps, st0, st1, of, om, w_conv, b_conv, wupc, wupf, wupm, wo, g_post)


def _head_slots(w):
    d = w.shape[0]
    w = w.reshape(d, H_F, HD_F)
    return jnp.pad(w, ((0, 0), (0, 0), (0, SLOT - HD_F))).reshape(d, H_F * SLOT)


def kernel(x_prompt, x_sample, cache_fox_k, cache_fox_v, cache_fox_logf, state_conv,
           cache_mem_k, cache_mem_v, page_table, mem_prompt,
           g_pre, w_in, b_f, w_conv, b_conv, g_mem, w_mem_kv,
           w_up_conv, w_up_fox, w_up_mem, w_o, g_post):
    bsz, seq, _ = x_prompt.shape
    assert bsz == 1 and seq % BLK == 0 and BLK % TM == 0
    nb, n_new, _ = x_sample.shape
    assert n_new <= T8 and page_table.shape[1] % PAGES_PER_STEP == 0
    assert cache_fox_k.shape[1] == PAGE

    wb = w_in.astype(BF16)
    o = D_MODEL
    w_conv4, w_fox4 = wb[:, 0:4 * o], wb[:, 4 * o:8 * o]
    w_f = wb[:, 8 * o:8 * o + H_F]
    w_rest = wb[:, 8 * o + H_F:]
    w_q, w_k, w_v, w_zf = (w_fox4[:, k * o:(k + 1) * o] for k in range(4))
    w_ga, w_gf, w_gm = (w_rest[:, (2 + k) * o:(3 + k) * o] for k in range(3))
    w_cm = jnp.concatenate([w_conv4, w_rest[:, 0:2 * o], w_ga, w_gm], axis=1)
    w_f_slot = jnp.pad(w_f, ((0, 0), (0, SLOT - H_F)))
    b_f_slot = jnp.pad(b_f, (0, SLOT - H_F)).reshape(1, SLOT)
    w_all = jnp.concatenate(
        [w_conv4, w_fox4, w_rest, jnp.pad(w_f, ((0, 0), (0, o - H_F)))], axis=1)
    b_f_seg = jnp.pad(b_f, (0, o - H_F)).reshape(1, o)
    wupc, wupf, wupm, wo = (w.astype(BF16) for w in (w_up_conv, w_up_fox, w_up_mem, w_o))
    g_pre2, g_mem2, g_post2 = (g.reshape(1, o) for g in (g_pre, g_mem, g_post))
    b_conv2 = b_conv.reshape(1, o)
    eq, ek = _decay_placement()

    xp = x_prompt.reshape(seq, o)
    mkv, mkv_b = _mem_kv(mem_prompt.reshape(N_MEM, o), g_mem2, w_mem_kv.astype(BF16))
    mcm, cstate = _branch_cm(xp, g_pre2, w_cm, w_conv, b_conv2,
                             mkv_b[:, :o], mkv_b[:, o:], wupc, wupm)
    k_p, v_p, qx, kx, vt, zf, sgf, logf_p = _proj_fox(
        xp, g_pre2, _head_slots(w_q), w_k, _head_slots(w_k), w_v, w_v.T, w_zf, w_gf,
        w_f_slot, b_f_slot, eq, ek)
    o_f = _attn_prompt(qx, kx, vt)
    y_p = _final_prompt(xp, mcm, o_f, zf, sgf, wupf, wo, g_post2)

    xs = x_sample.reshape(nb * n_new, o)
    ps = _proj_sample(xs, g_pre2, w_all, b_f_seg)

    def seg(k):
        return ps[:, k * o:(k + 1) * o]

    def pad8(a):
        a = a.reshape(nb, n_new, a.shape[-1])
        return jnp.pad(a, ((0, 0), (0, T8 - n_new), (0, 0)))

    k_s, v_s = seg(5), seg(6)
    logf_s = ps[:, SEG_FORGET * o:SEG_FORGET * o + H_F]
    n_phys = cache_fox_k.shape[0]
    of8 = _attn_sample(page_table, pad8(seg(4)), pad8(k_s), pad8(v_s), pad8(logf_s),
                       cache_fox_k.reshape(n_phys, PAGE, o),
                       cache_fox_v.reshape(n_phys, PAGE, o),
                       cache_fox_logf, n_new)
    om8 = _mem_sample(pad8(seg(8)), cache_mem_k.reshape(nb, N_MEM, o),
                      cache_mem_v.reshape(nb, N_MEM, o))
    st0 = jnp.repeat(state_conv[:, 0, :], n_new, axis=0)
    st1 = jnp.repeat(state_conv[:, 1, :], n_new, axis=0)
    y_s, u_s = _merge_sample(xs, ps, st0, st1,
                             of8[:, :n_new].reshape(nb * n_new, o),
                             om8[:, :n_new].reshape(nb * n_new, o),
                             w_conv, b_conv2, wupc, wupf, wupm, wo, g_post2, n_new)

    return (y_p.reshape(1, seq, o),
            y_s.reshape(nb, n_new, o),
            k_p.reshape(1, seq, H_F, HD_F),
            v_p.reshape(1, seq, H_F, HD_F),
            logf_p.reshape(1, seq, H_F),
            cstate.reshape(1, 2, o),
            mkv[:, :o].reshape(1, N_MEM, H_M, HD_M),
            mkv[:, o:].reshape(1, N_MEM, H_M, HD_M),
            k_s.reshape(nb, n_new, H_F, HD_F),
            v_s.reshape(nb, n_new, H_F, HD_F),
            logf_s.reshape(nb, n_new, H_F),
            u_s.reshape(nb, n_new, o)[:, n_new - 2:, :])
```

```python
import functools

import numpy as np
import jax
import jax.numpy as jnp
from jax import lax
from jax.experimental import pallas as pl
from jax.experimental.pallas import tpu as pltpu

D_MODEL = 1024
H_F = 16
HD_F = 64
H_M = 4
HD_M = 256
N_MEM = 256
PAGE = 128
RMS_EPS = 1e-6
LOG2E = 1.4426950408889634
FOX_QSCALE = (HD_F ** -0.5) * LOG2E
MEM_SCALE = HD_M ** -0.5

F32 = jnp.float32
BF16 = jnp.bfloat16

TM = 256
BLK = 512
SUB = 256
LOOKAHEAD = 2
PAGES_PER_STEP = 8
SLOT = 128
VROWS = 80

VMEM_LIMIT = 56 * 1024 * 1024

_NT = (((1,), (1,)), ((), ()))


def _dot(a, b):
    return jnp.dot(a, b, preferred_element_type=F32)


def _dot_nt(a, b):
    return lax.dot_general(a, b, _NT, preferred_element_type=F32)


def _rms(x, g):
    ms = jnp.mean(x * x, axis=-1, keepdims=True)
    return x * lax.rsqrt(ms + RMS_EPS) * g


def _sigmoid(x):
    return 1.0 / (1.0 + jnp.exp(-x))


def _silu(x):
    return x * _sigmoid(x)


def _log_sigmoid(x):
    return -(jnp.maximum(-x, 0.0) + jnp.log1p(jnp.exp(-jnp.abs(x))))


def _split3(x):
    hi = x.astype(BF16)
    r = x - hi.astype(F32)
    mid = r.astype(BF16)
    lo = (r - mid.astype(F32)).astype(BF16)
    return hi, mid, lo


def _lower_tri(n):
    row = lax.broadcasted_iota(jnp.int32, (n, n), 0)
    col = lax.broadcasted_iota(jnp.int32, (n, n), 1)
    return (col <= row).astype(BF16)


def _cumsum_rows(x):
    tri = _lower_tri(x.shape[0])
    a, b, c = _split3(x)
    return _dot(tri, a) + _dot(tri, b) + _dot(tri, c)


def _const_spec(shape):
    zeros = (0,) * len(shape)
    return pl.BlockSpec(shape, lambda *_: zeros, pipeline_mode=pl.Buffered(1))


def _memkv_kernel(mem_ref, g_ref, w_ref, kv_ref, kvb_ref):
    n = _rms(mem_ref[...], g_ref[...]).astype(BF16)
    kv = _dot(n, w_ref[...])
    kv_ref[...] = kv
    kvb_ref[...] = kv.astype(BF16)


def _mem_kv(mem, g_mem, w_kv_b):
    return pl.pallas_call(
        _memkv_kernel,
        out_shape=(jax.ShapeDtypeStruct((N_MEM, 2 * D_MODEL), F32),
                   jax.ShapeDtypeStruct((N_MEM, 2 * D_MODEL), BF16)),
        compiler_params=pltpu.CompilerParams(vmem_limit_bytes=VMEM_LIMIT),
        name="mem_kv",
    )(mem, g_mem, w_kv_b)


def _mem_attention(qm, mk_ref, mv_ref):
    parts = []
    for hh in range(H_M):
        sl = slice(hh * HD_M, (hh + 1) * HD_M)
        s = _dot_nt(qm[:, sl].astype(BF16), mk_ref[:, sl]) * MEM_SCALE
        s = s - jnp.max(s, axis=-1, keepdims=True)
        p = jnp.exp(s)
        p = p / jnp.sum(p, axis=-1, keepdims=True)
        parts.append(_dot(p.astype(BF16), mv_ref[:, sl]))
    return jnp.concatenate(parts, axis=1)


def _branch_cm_kernel(x_ref, g_ref, w_ref, wconv_ref, bconv_ref, mk_ref, mv_ref,
                      wupc_ref, wupm_ref, mcm_ref, cstate_ref, ubuf):
    i = pl.program_id(0)

    @pl.when(i == 0)
    def _():
        ubuf[0:8, :] = jnp.zeros((8, D_MODEL), F32)

    h = _rms(x_ref[...], g_ref[...]).astype(BF16)

    def seg(k):
        return _dot(h, w_ref[:, k * D_MODEL:(k + 1) * D_MODEL])

    u = seg(2) * seg(0)
    ubuf[8:8 + TM, :] = u
    u1 = ubuf[7:7 + TM, :]
    u2 = ubuf[6:6 + TM, :]
    wc = wconv_ref[...]
    conv = wc[0:1, :] * u2 + wc[1:2, :] * u1 + wc[2:3, :] * u + bconv_ref[...]
    cstate_ref[...] = ubuf[TM + 6:TM + 8, :]
    ubuf[0:8, :] = ubuf[TM:TM + 8, :]
    o_c = seg(1) * conv * _silu(seg(3))
    m = _sigmoid(seg(6)) * _dot(o_c.astype(BF16), wupc_ref[...])

    o_m = _mem_attention(seg(4), mk_ref, mv_ref) * _silu(seg(5))
    m = m + _sigmoid(seg(7)) * _dot(o_m.astype(BF16), wupm_ref[...])
    mcm_ref[...] = m


def _branch_cm(x, g_pre, w_cm, w_conv, b_conv, mk_b, mv_b, wupc, wupm):
    s = x.shape[0]
    row = lambda i: (i, 0)
    return pl.pallas_call(
        _branch_cm_kernel,
        grid=(s // TM,),
        in_specs=[
            pl.BlockSpec((TM, D_MODEL), row),
            _const_spec((1, D_MODEL)),
            _const_spec((D_MODEL, 8 * D_MODEL)),
            _const_spec((3, D_MODEL)),
            _const_spec((1, D_MODEL)),
            _const_spec((N_MEM, D_MODEL)),
            _const_spec((N_MEM, D_MODEL)),
            _const_spec((D_MODEL, D_MODEL)),
            _const_spec((D_MODEL, D_MODEL)),
        ],
        out_specs=(pl.BlockSpec((TM, D_MODEL), row),
                   pl.BlockSpec((2, D_MODEL), lambda i: (0, 0))),
        out_shape=(jax.ShapeDtypeStruct((s, D_MODEL), F32),
                   jax.ShapeDtypeStruct((2, D_MODEL), F32)),
        scratch_shapes=[pltpu.VMEM((TM + 8, D_MODEL), F32)],
        compiler_params=pltpu.CompilerParams(
            dimension_semantics=("arbitrary",), vmem_limit_bytes=VMEM_LIMIT),
        name="prompt_branch_cm",
    )(x, g_pre, w_cm, w_conv, b_conv, mk_b, mv_b, wupc, wupm)


def _decay_placement():
    eq = np.zeros((3, SLOT, H_F * SLOT), np.float32)
    ek = np.zeros((3, SLOT, H_F * SLOT), np.float32)
    for h in range(H_F):
        base = h * SLOT + HD_F
        for part in range(3):
            eq[part, h, base + part] = 1.0
            ek[part, h, base + 3 + part] = 1.0
        eq[0, H_F, base + 3:base + 6] = 1.0
        ek[0, H_F, base:base + 3] = 1.0
    return jnp.asarray(eq, BF16), jnp.asarray(ek, BF16)


def _proj_fox_kernel(x_ref, g_ref, wqp_ref, wk_ref, wkp_ref, wv_ref, wvt_ref, vone_ref,
                     wzf_ref, wgf_ref, wf_ref, bf_ref, eq_ref, ek_ref,
                     k_ref, v_ref, qx_ref, kx_ref, vt_ref, zf_ref, sgf_ref,
                     logf_ref, carry):
    i = pl.program_id(0)

    @pl.when(i == 0)
    def _():
        carry[...] = jnp.zeros((1, SLOT), F32)

    h = _rms(x_ref[...], g_ref[...]).astype(BF16)
    k_ref[...] = _dot(h, wk_ref[...])
    v_ref[...] = _dot(h, wv_ref[...])
    vt_ref[...] = (_dot_nt(wvt_ref[...], h) + vone_ref[...]).astype(BF16)
    zf_ref[...] = _dot(h, wzf_ref[...]).astype(BF16)
    sgf_ref[...] = _sigmoid(_dot(h, wgf_ref[...])).astype(BF16)

    lf = _log_sigmoid(_dot(h, wf_ref[...]) + bf_ref[...])
    logf_ref[...] = lf[:, 0:H_F]
    dcum = _cumsum_rows(lf) + carry[...]
    carry[...] = dcum[TM - 1:TM, :]
    dh, dm, dl = _split3(dcum * LOG2E)
    one_lane = lax.broadcasted_iota(jnp.int32, (TM, SLOT), 1) == H_F
    one = jnp.ones((TM, SLOT), BF16)

    q = _dot(h, wqp_ref[...]) * FOX_QSCALE
    q = (q + _dot(jnp.where(one_lane, one, dh), eq_ref[0])
         + _dot(dm, eq_ref[1]) + _dot(dl, eq_ref[2]))
    kp = _dot(h, wkp_ref[...])
    kp = (kp + _dot(jnp.where(one_lane, one, -dh), ek_ref[0])
          + _dot(-dm, ek_ref[1]) + _dot(-dl, ek_ref[2]))
    for hh in range(H_F):
        sl = slice(hh * SLOT, (hh + 1) * SLOT)
        qx_ref[hh] = q[:, sl].astype(BF16)
        kx_ref[hh] = kp[:, sl].astype(BF16)


def _proj_fox(x, g_pre, wqp, wk, wkp, wv, wvt, vone, wzf, wgf, wf, bf, eq, ek):
    s = x.shape[0]
    row = lambda i: (i, 0)
    slot = lambda i: (0, i, 0)
    return pl.pallas_call(
        _proj_fox_kernel,
        grid=(s // TM,),
        in_specs=[
            pl.BlockSpec((TM, D_MODEL), row),
            _const_spec((1, D_MODEL)),
            _const_spec((D_MODEL, H_F * SLOT)),
            _const_spec((D_MODEL, D_MODEL)),
            _const_spec((D_MODEL, H_F * SLOT)),
            _const_spec((D_MODEL, D_MODEL)),
            _const_spec((H_F * VROWS, D_MODEL)),
            _const_spec((H_F * VROWS, 1)),
            _const_spec((D_MODEL, D_MODEL)),
            _const_spec((D_MODEL, D_MODEL)),
            _const_spec((D_MODEL, SLOT)),
            _const_spec((1, SLOT)),
            _const_spec((3, SLOT, H_F * SLOT)),
            _const_spec((3, SLOT, H_F * SLOT)),
        ],
        out_specs=(
            pl.BlockSpec((TM, D_MODEL), row),
            pl.BlockSpec((TM, D_MODEL), row),
            pl.BlockSpec((H_F, TM, SLOT), slot),
            pl.BlockSpec((H_F, TM, SLOT), slot),
            pl.BlockSpec((H_F * VROWS, TM), lambda i: (0, i)),
            pl.BlockSpec((TM, D_MODEL), row),
            pl.BlockSpec((TM, D_MODEL), row),
            pl.BlockSpec((TM, H_F), row),
        ),
        out_shape=(
            jax.ShapeDtypeStruct((s, D_MODEL), F32),
            jax.ShapeDtypeStruct((s, D_MODEL), F32),
            jax.ShapeDtypeStruct((H_F, s, SLOT), BF16),
            jax.ShapeDtypeStruct((H_F, s, SLOT), BF16),
            jax.ShapeDtypeStruct((H_F * VROWS, s), BF16),
            jax.ShapeDtypeStruct((s, D_MODEL), BF16),
            jax.ShapeDtypeStruct((s, D_MODEL), BF16),
            jax.ShapeDtypeStruct((s, H_F), F32),
        ),
        scratch_shapes=[pltpu.VMEM((1, SLOT), F32)],
        compiler_params=pltpu.CompilerParams(
            dimension_semantics=("arbitrary",), vmem_limit_bytes=VMEM_LIMIT),
        name="prompt_proj_fox",
    )(x, g_pre, wqp, wk, wkp, wv, wvt, vone, wzf, wgf, wf, bf, eq, ek)


def _attn_kernel(it_ref, jt_ref, qx_ref, kx_ref, vt_ref, o_ref, m_sc, acc_sc):
    step = pl.program_id(0)
    i = it_ref[step]
    j = jt_ref[step]

    @pl.when(j == 0)
    def _():
        m_sc[...] = jnp.full((H_F, BLK), -jnp.inf, F32)
        acc_sc[...] = jnp.zeros((H_F * VROWS, BLK), F32)

    def colmax(a):
        half = a.shape[0] // 2
        return jnp.maximum(jnp.max(a[:half], axis=0, keepdims=True),
                           jnp.max(a[half:], axis=0, keepdims=True))

    def logits(sub, hh):
        return _dot_nt(kx_ref[hh, sub * SUB:(sub + 1) * SUB, :], qx_ref[hh])

    def block(masked):
        units = [(sub, hh) for sub in range(BLK // SUB) for hh in range(H_F)]
        qry = lax.broadcasted_iota(jnp.int32, (SUB, BLK), 1)
        key = lax.broadcasted_iota(jnp.int32, (SUB, BLK), 0)
        pending = [logits(*u) for u in units[:LOOKAHEAD]]
        for n, (sub, hh) in enumerate(units):
            if n + LOOKAHEAD < len(units):
                pending.append(logits(*units[n + LOOKAHEAD]))
            st = pending.pop(0)
            if masked:
                st = jnp.where(key + sub * SUB <= qry, st, -jnp.inf)
            m_old = m_sc[hh:hh + 1, :]
            m_new = jnp.maximum(m_old, colmax(st))
            alpha = jnp.exp2(m_old - m_new)
            p = jnp.exp2(st - m_new)
            m_sc[hh:hh + 1, :] = m_new
            rows = slice(hh * VROWS, (hh + 1) * VROWS)
            pv = _dot(vt_ref[rows, sub * SUB:(sub + 1) * SUB], p.astype(BF16))
            acc_sc[rows, :] = alpha * acc_sc[rows, :] + pv

    @pl.when(j < i)
    def _():
        block(False)

    @pl.when(j == i)
    def _():
        block(True)
        heads = []
        for hh in range(H_F):
            base = hh * VROWS
            inv = 1.0 / acc_sc[base + HD_F:base + HD_F + 1, :]
            heads.append(acc_sc[base:base + HD_F, :] * inv)
        o_ref[...] = jnp.concatenate(heads, axis=0).T.astype(BF16)


def _attn_prompt(qx, kx, vt):
    s = qx.shape[1]
    nb = s // BLK
    it = np.concatenate([np.full(i + 1, i) for i in range(nb)]).astype(np.int32)
    jt = np.concatenate([np.arange(i + 1) for i in range(nb)]).astype(np.int32)
    grid_spec = pltpu.PrefetchScalarGridSpec(
        num_scalar_prefetch=2,
        grid=(it.shape[0],),
        in_specs=[
            pl.BlockSpec((H_F, BLK, SLOT), lambda t, it, jt: (0, it[t], 0)),
            pl.BlockSpec((H_F, BLK, SLOT), lambda t, it, jt: (0, jt[t], 0)),
            pl.BlockSpec((H_F * VROWS, BLK), lambda t, it, jt: (0, jt[t])),
        ],
        out_specs=pl.BlockSpec((BLK, D_MODEL), lambda t, it, jt: (it[t], 0)),
        scratch_shapes=[pltpu.VMEM((H_F, BLK), F32),
                        pltpu.VMEM((H_F * VROWS, BLK), F32)],
    )
    return pl.pallas_call(
        _attn_kernel,
        grid_spec=grid_spec,
        out_shape=jax.ShapeDtypeStruct((s, D_MODEL), BF16),
        compiler_params=pltpu.CompilerParams(
            dimension_semantics=("arbitrary",), vmem_limit_bytes=VMEM_LIMIT),
        name="prompt_fox_attention",
    )(jnp.asarray(it), jnp.asarray(jt), qx, kx, vt)


def _final_kernel(x_ref, mcm_ref, of_ref, zf_ref, sgf_ref, wupf_ref, wo_ref, g_ref, y_ref):
    og = (of_ref[...].astype(F32) * _silu(zf_ref[...].astype(F32))).astype(BF16)
    m = mcm_ref[...] + sgf_ref[...].astype(F32) * _dot(og, wupf_ref[...])
    y_ref[...] = x_ref[...] + _rms(_dot(m.astype(BF16), wo_ref[...]), g_ref[...])


def _final_prompt(x, mcm, of, zf, sgf, wupf, wo, g_post):
    s = x.shape[0]
    row = lambda i: (i, 0)
    tile = pl.BlockSpec((TM, D_MODEL), row)
    return pl.pallas_call(
        _final_kernel,
        grid=(s // TM,),
        in_specs=[tile, tile, tile, tile, tile,
                  _const_spec((D_MODEL, D_MODEL)),
                  _const_spec((D_MODEL, D_MODEL)),
                  _const_spec((1, D_MODEL))],
        out_specs=tile,
        out_shape=jax.ShapeDtypeStruct((s, D_MODEL), F32),
        compiler_params=pltpu.CompilerParams(
            dimension_semantics=("arbitrary",), vmem_limit_bytes=VMEM_LIMIT),
        name="prompt_merge_out",
    )(x, mcm, of, zf, sgf, wupf, wo, g_post)


N_SEG = 14
SEG_FORGET = 13


def _proj_sample_kernel(x_ref, g_ref, w_ref, bf_ref, p_ref):
    j = pl.program_id(0)
    h = _rms(x_ref[...], g_ref[...]).astype(BF16)
    acc = _dot(h, w_ref[...])

    @pl.when(j != SEG_FORGET)
    def _():
        p_ref[...] = acc

    @pl.when(j == SEG_FORGET)
    def _():
        p_ref[...] = _log_sigmoid(acc + bf_ref[...])


def _proj_sample(x, g_pre, w_all, bf_pad):
    n = x.shape[0]
    return pl.pallas_call(
        _proj_sample_kernel,
        grid=(N_SEG,),
        in_specs=[pl.BlockSpec((n, D_MODEL), lambda j: (0, 0)),
                  pl.BlockSpec((1, D_MODEL), lambda j: (0, 0)),
                  pl.BlockSpec((D_MODEL, D_MODEL), lambda j: (0, j)),
                  pl.BlockSpec((1, D_MODEL), lambda j: (0, 0))],
        out_specs=pl.BlockSpec((n, D_MODEL), lambda j: (0, j)),
        out_shape=jax.ShapeDtypeStruct((n, N_SEG * D_MODEL), F32),
        compiler_params=pltpu.CompilerParams(
            dimension_semantics=("arbitrary",), vmem_limit_bytes=VMEM_LIMIT),
        name="sample_proj",
    )(x, g_pre, w_all, bf_pad)


T8 = 8


def _upper_tri(n):
    row = lax.broadcasted_iota(jnp.int32, (n, n), 0)
    col = lax.broadcasted_iota(jnp.int32, (n, n), 1)
    return (row <= col).astype(BF16)


def _attn_sample_kernel(pt_ref, q_ref, kn_ref, vn_ref, lfn_ref, *refs, n_new):
    pp = PAGES_PER_STEP
    k_refs = refs[0:pp]
    v_refs = refs[pp:2 * pp]
    lf_refs = refs[2 * pp:3 * pp]
    o_ref = refs[3 * pp]
    qbd, kbuf, vbuf, bias, m_sc, l_sc, acc_sc, carry = refs[3 * pp + 1:]
    rows = n_new * H_F
    c = pl.program_id(1)
    own = (lax.broadcasted_iota(jnp.int32, (H_F, D_MODEL), 1) // HD_F
           == lax.broadcasted_iota(jnp.int32, (H_F, D_MODEL), 0))

    @pl.when(c == 0)
    def _():
        q = q_ref[0] * FOX_QSCALE
        qbd[...] = jnp.concatenate(
            [jnp.where(own, q[t:t + 1, :], 0.0) for t in range(n_new)], axis=0).astype(BF16)
        m_sc[...] = jnp.full((rows, 1), -jnp.inf, F32)
        l_sc[...] = jnp.zeros((rows, 1), F32)
        acc_sc[...] = jnp.zeros((rows, D_MODEL), F32)
        carry[...] = jnp.zeros((H_F, 1), F32)

    tri = _upper_tri(PAGE)

    def decay_bias(lf_t, base):
        a, b, d = _split3(lf_t)
        dcum = _dot(a, tri) + _dot(b, tri) + _dot(d, tri) + base
        nb = -(dcum * LOG2E)
        return dcum[:, PAGE - 1:PAGE], jnp.concatenate([nb] * n_new, axis=0)

    def update(s, pv_fn):
        m_old = m_sc[...]
        m_new = jnp.maximum(m_old, jnp.max(s, axis=1, keepdims=True))
        alpha = jnp.exp2(m_old - m_new)
        p = jnp.exp2(s - m_new)
        l_sc[...] = alpha * l_sc[...] + jnp.sum(p, axis=1, keepdims=True)
        acc_sc[...] = alpha * acc_sc[...] + pv_fn(p.astype(BF16))
        m_sc[...] = m_new

    base = carry[...]
    for r in range(pp):
        cols = slice(r * PAGE, (r + 1) * PAGE)
        kbuf[:, cols] = k_refs[r][0].astype(BF16)
        vbuf[:, cols] = v_refs[r][0].astype(BF16)
        base, bias[:, cols] = decay_bias(lf_refs[r][0], base)
    carry[...] = base
    update(_dot(qbd[...], kbuf[...]) + bias[...], lambda p: _dot_nt(p, vbuf[...]))

    @pl.when(c == pl.num_programs(1) - 1)
    def _():
        pad = jnp.zeros((PAGE - T8, D_MODEL), F32)
        kn = jnp.concatenate([kn_ref[0], pad], axis=0).astype(BF16)
        vn = jnp.concatenate([vn_ref[0], pad], axis=0).astype(BF16)
        _, b_new = decay_bias(lfn_ref[0], carry[...])
        s = _dot_nt(qbd[...], kn) + b_new
        tok = lax.broadcasted_iota(jnp.int32, (rows, PAGE), 1)
        t_row = lax.broadcasted_iota(jnp.int32, (rows, PAGE), 0) // H_F
        s = jnp.where((tok < n_new) & (tok <= t_row), s, -jnp.inf)
        update(s, lambda p: _dot(p, vn))
        out = acc_sc[...] * (1.0 / l_sc[...])
        o_ref[0] = jnp.zeros((T8, D_MODEL), F32)
        for t in range(n_new):
            piece = jnp.where(own, out[t * H_F:(t + 1) * H_F, :], 0.0)
            o_ref[0, t:t + 1, :] = jnp.sum(piece, axis=0, keepdims=True)


def _attn_sample(page_table, q, kn8, vn8, lfn_t, cache_kt, cache_vt, cache_lft, n_new):
    nb, n_pages = page_table.shape
    pp = PAGES_PER_STEP
    rows = n_new * H_F
    per_b = lambda b, c, pt: (b, 0, 0)

    def page_map(r):
        return lambda b, c, pt: (pt[b * n_pages + c * pp + r], 0, 0)

    in_specs = [pl.BlockSpec((1, n_new, D_MODEL), per_b),
                pl.BlockSpec((1, T8, D_MODEL), per_b),
                pl.BlockSpec((1, T8, D_MODEL), per_b),
                pl.BlockSpec((1, H_F, PAGE), per_b)]
    in_specs += [pl.BlockSpec((1, D_MODEL, PAGE), page_map(r)) for r in range(pp)]
    in_specs += [pl.BlockSpec((1, D_MODEL, PAGE), page_map(r)) for r in range(pp)]
    in_specs += [pl.BlockSpec((1, H_F, PAGE), page_map(r)) for r in range(pp)]
    grid_spec = pltpu.PrefetchScalarGridSpec(
        num_scalar_prefetch=1,
        grid=(nb, n_pages // pp),
        in_specs=in_specs,
        out_specs=pl.BlockSpec((1, T8, D_MODEL), per_b),
        scratch_shapes=[
            pltpu.VMEM((rows, D_MODEL), BF16),
            pltpu.VMEM((D_MODEL, pp * PAGE), BF16),
            pltpu.VMEM((D_MODEL, pp * PAGE), BF16),
            pltpu.VMEM((rows, pp * PAGE), F32),
            pltpu.VMEM((rows, 1), F32),
            pltpu.VMEM((rows, 1), F32),
            pltpu.VMEM((rows, D_MODEL), F32),
            pltpu.VMEM((H_F, 1), F32),
        ],
    )
    args = [page_table.reshape(-1), q, kn8, vn8, lfn_t]
    args += [cache_kt] * pp + [cache_vt] * pp + [cache_lft] * pp
    return pl.pallas_call(
        functools.partial(_attn_sample_kernel, n_new=n_new),
        grid_spec=grid_spec,
        out_shape=jax.ShapeDtypeStruct((nb, T8, D_MODEL), F32),
        compiler_params=pltpu.CompilerParams(
            dimension_semantics=("arbitrary", "arbitrary"), vmem_limit_bytes=VMEM_LIMIT),
        name="sample_fox_attention",
    )(*args)


ROWS_M = H_M * T8


def _mem_sample_kernel(q_ref, k_ref, v_ref, o_ref):
    lane_head = lax.broadcasted_iota(jnp.int32, (T8, D_MODEL), 1) // HD_M
    q8 = q_ref[0]
    qbd = jnp.concatenate(
        [jnp.where(lane_head == hh, q8, 0.0) for hh in range(H_M)], axis=0).astype(BF16)
    s = _dot_nt(qbd, k_ref[0].astype(BF16)) * MEM_SCALE
    s = s - jnp.max(s, axis=-1, keepdims=True)
    p = jnp.exp(s)
    p = p / jnp.sum(p, axis=-1, keepdims=True)
    out = _dot(p.astype(BF16), v_ref[0].astype(BF16))
    o8 = jnp.zeros((T8, D_MODEL), F32)
    for hh in range(H_M):
        o8 = o8 + jnp.where(lane_head == hh, out[hh * T8:(hh + 1) * T8, :], 0.0)
    o_ref[0] = o8


def _mem_sample(qm8, mem_k, mem_v):
    nb = qm8.shape[0]
    per_b = lambda b: (b, 0, 0)
    return pl.pallas_call(
        _mem_sample_kernel,
        grid=(nb,),
        in_specs=[pl.BlockSpec((1, T8, D_MODEL), per_b),
                  pl.BlockSpec((1, N_MEM, D_MODEL), per_b),
                  pl.BlockSpec((1, N_MEM, D_MODEL), per_b)],
        out_specs=pl.BlockSpec((1, T8, D_MODEL), per_b),
        out_shape=jax.ShapeDtypeStruct((nb, T8, D_MODEL), F32),
        compiler_params=pltpu.CompilerParams(
            dimension_semantics=("arbitrary",), vmem_limit_bytes=VMEM_LIMIT),
        name="sample_mem_attention",
    )(qm8, mem_k, mem_v)


def _merge_sample_kernel(x_ref, p_ref, st0_ref, st1_ref, of_ref, om_ref, wconv_ref, bconv_ref,
                         wupc_ref, wupf_ref, wupm_ref, wo_ref, g_ref, y_ref, u_ref, ubuf,
                         *, n_new):
    n = x_ref.shape[0]

    def seg(k):
        return p_ref[:, k * D_MODEL:(k + 1) * D_MODEL]

    u = seg(2) * seg(0)
    u_ref[...] = u
    ubuf[0:8, :] = jnp.zeros((8, D_MODEL), F32)
    ubuf[8:8 + n, :] = u
    t = lax.broadcasted_iota(jnp.int32, (n, D_MODEL), 0) % n_new
    u1 = jnp.where(t == 0, st1_ref[...], ubuf[7:7 + n, :])
    u2 = jnp.where(t == 0, st0_ref[...], jnp.where(t == 1, st1_ref[...], ubuf[6:6 + n, :]))
    wc = wconv_ref[...]
    conv = wc[0:1, :] * u2 + wc[1:2, :] * u1 + wc[2:3, :] * u + bconv_ref[...]
    o_c = seg(1) * conv * _silu(seg(3))
    o_f = of_ref[...] * _silu(seg(7))
    o_m = om_ref[...] * _silu(seg(9))
    m = (_sigmoid(seg(10)) * _dot(o_c.astype(BF16), wupc_ref[...])
         + _sigmoid(seg(11)) * _dot(o_f.astype(BF16), wupf_ref[...])
         + _sigmoid(seg(12)) * _dot(o_m.astype(BF16), wupm_ref[...]))
    y_ref[...] = x_ref[...] + _rms(_dot(m.astype(BF16), wo_ref[...]), g_ref[...])


def _merge_sample(x, ps, st0, st1, of, om, w_conv, b_conv, wupc, wupf, wupm, wo, g_post, n_new):
    n = x.shape[0]
    return pl.pallas_call(
        functools.partial(_merge_sample_kernel, n_new=n_new),
        out_shape=(jax.ShapeDtypeStruct((n, D_MODEL), F32),
                   jax.ShapeDtypeStruct((n, D_MODEL), F32)),
        scratch_shapes=[pltpu.VMEM((n + 8, D_MODEL), F32)],
        compiler_params=pltpu.CompilerParams(vmem_limit_bytes=VMEM_LIMIT),
        name="sample_merge_out",
    )(x, ps, st0, st1, of, om, w_conv, b_conv, wupc, wupf, wupm, wo, g_post)


def _head_slots(w):
    d = w.shape[0]
    w = w.reshape(d, H_F, HD_F)
    return jnp.pad(w, ((0, 0), (0, 0), (0, SLOT - HD_F))).reshape(d, H_F * SLOT)


def kernel(x_prompt, x_sample, cache_fox_k, cache_fox_v, cache_fox_logf, state_conv,
           cache_mem_k, cache_mem_v, page_table, mem_prompt,
           g_pre, w_in, b_f, w_conv, b_conv, g_mem, w_mem_kv,
           w_up_conv, w_up_fox, w_up_mem, w_o, g_post):
    bsz, seq, _ = x_prompt.shape
    assert bsz == 1 and seq % BLK == 0 and BLK % TM == 0
    nb, n_new, _ = x_sample.shape
    assert n_new <= T8 and page_table.shape[1] % PAGES_PER_STEP == 0
    assert cache_fox_k.shape[1] == PAGE

    wb = w_in.astype(BF16)
    o = D_MODEL
    w_conv4, w_fox4 = wb[:, 0:4 * o], wb[:, 4 * o:8 * o]
    w_f = wb[:, 8 * o:8 * o + H_F]
    w_rest = wb[:, 8 * o + H_F:]
    w_q, w_k, w_v, w_zf = (w_fox4[:, k * o:(k + 1) * o] for k in range(4))
    w_ga, w_gf, w_gm = (w_rest[:, (2 + k) * o:(3 + k) * o] for k in range(3))
    w_cm = jnp.concatenate([w_conv4, w_rest[:, 0:2 * o], w_ga, w_gm], axis=1)
    w_f_slot = jnp.pad(w_f, ((0, 0), (0, SLOT - H_F)))
    b_f_slot = jnp.pad(b_f, (0, SLOT - H_F)).reshape(1, SLOT)
    w_all = jnp.concatenate(
        [w_conv4, w_fox4, w_rest, jnp.pad(w_f, ((0, 0), (0, o - H_F)))], axis=1)
    b_f_seg = jnp.pad(b_f, (0, o - H_F)).reshape(1, o)
    wupc, wupf, wupm, wo = (w.astype(BF16) for w in (w_up_conv, w_up_fox, w_up_mem, w_o))
    g_pre2, g_mem2, g_post2 = (g.reshape(1, o) for g in (g_pre, g_mem, g_post))
    b_conv2 = b_conv.reshape(1, o)
    eq, ek = _decay_placement()
    w_vt = jnp.pad(w_v.T.reshape(H_F, HD_F, o), ((0, 0), (0, VROWS - HD_F), (0, 0)))
    w_vt = w_vt.reshape(H_F * VROWS, o)
    v_one = jnp.asarray((np.arange(H_F * VROWS) % VROWS == HD_F).astype(np.float32)[:, None])

    xp = x_prompt.reshape(seq, o)
    mkv, mkv_b = _mem_kv(mem_prompt.reshape(N_MEM, o), g_mem2, w_mem_kv.astype(BF16))
    mcm, cstate = _branch_cm(xp, g_pre2, w_cm, w_conv, b_conv2,
                             mkv_b[:, :o], mkv_b[:, o:], wupc, wupm)
    k_p, v_p, qx, kx, vt, zf, sgf, logf_p = _proj_fox(
        xp, g_pre2, _head_slots(w_q), w_k, _head_slots(w_k), w_v, w_vt, v_one, w_zf, w_gf,
        w_f_slot, b_f_slot, eq, ek)
    o_f = _attn_prompt(qx, kx, vt)
    y_p = _final_prompt(xp, mcm, o_f, zf, sgf, wupf, wo, g_post2)

    xs = x_sample.reshape(nb * n_new, o)
    ps = _proj_sample(xs, g_pre2, w_all, b_f_seg)

    def seg(k):
        return ps[:, k * o:(k + 1) * o]

    def pad8(a):
        a = a.reshape(nb, n_new, a.shape[-1])
        return jnp.pad(a, ((0, 0), (0, T8 - n_new), (0, 0)))

    k_s, v_s = seg(5), seg(6)
    logf_s = ps[:, SEG_FORGET * o:SEG_FORGET * o + H_F]
    n_phys = cache_fox_k.shape[0]
    cache_kt = jnp.transpose(cache_fox_k, (0, 2, 3, 1)).reshape(n_phys, o, PAGE)
    cache_vt = jnp.transpose(cache_fox_v, (0, 2, 3, 1)).reshape(n_phys, o, PAGE)
    cache_lft = jnp.transpose(cache_fox_logf, (0, 2, 1))
    lfn_t = jnp.pad(jnp.transpose(logf_s.reshape(nb, n_new, H_F), (0, 2, 1)),
                    ((0, 0), (0, 0), (0, PAGE - n_new)))
    of8 = _attn_sample(page_table, seg(4).reshape(nb, n_new, o), pad8(k_s), pad8(v_s), lfn_t,
                       cache_kt, cache_vt, cache_lft, n_new)
    om8 = _mem_sample(pad8(seg(8)), cache_mem_k.reshape(nb, N_MEM, o),
                      cache_mem_v.reshape(nb, N_MEM, o))
    st0 = jnp.repeat(state_conv[:, 0, :], n_new, axis=0)
    st1 = jnp.repeat(state_conv[:, 1, :], n_new, axis=0)
    y_s, u_s = _merge_sample(xs, ps, st0, st1,
                             of8[:, :n_new].reshape(nb * n_new, o),
                             om8[:, :n_new].reshape(nb * n_new, o),
                             w_conv, b_conv2, wupc, wupf, wupm, wo, g_post2, n_new)

    return (y_p.reshape(1, seq, o),
            y_s.reshape(nb, n_new, o),
            k_p.reshape(1, seq, H_F, HD_F),
            v_p.reshape(1, seq, H_F, HD_F),
            logf_p.reshape(1, seq, H_F),
            cstate.reshape(1, 2, o),
            mkv[:, :o].reshape(1, N_MEM, H_M, HD_M),
            mkv[:, o:].reshape(1, N_MEM, H_M, HD_M),
            k_s.reshape(nb, n_new, H_F, HD_F),
            v_s.reshape(nb, n_new, H_F, HD_F),
            logf_s.reshape(nb, n_new, H_F),
            u_s.reshape(nb, n_new, o)[:, n_new - 2:, :])
```

```python
import functools

import numpy as np
import jax
import jax.numpy as jnp
from jax import lax
from jax.experimental import pallas as pl
from jax.experimental.pallas import tpu as pltpu

D_MODEL = 1024
H_F = 16
HD_F = 64
H_M = 4
HD_M = 256
N_MEM = 256
PAGE = 128
RMS_EPS = 1e-6
LOG2E = 1.4426950408889634
FOX_QSCALE = (HD_F ** -0.5) * LOG2E
MEM_SCALE = HD_M ** -0.5

F32 = jnp.float32
BF16 = jnp.bfloat16

TM = 256
BLK = 512
SUB = 256
LOOKAHEAD = 2
PAGES_PER_STEP = 16
SLOT = 128
VROWS = 80

(SEG_XC, SEG_BC, SEG_CC, SEG_ZC, SEG_Q, SEG_K, SEG_V, SEG_ZF,
 SEG_QM, SEG_ZM, SEG_GA, SEG_GF, SEG_GM, SEG_FORGET) = range(14)
N_SEG = 14

VMEM_LIMIT = 56 * 1024 * 1024

_NT = (((1,), (1,)), ((), ()))


def _dot(a, b):
    return jnp.dot(a, b, preferred_element_type=F32)


def _dot_nt(a, b):
    return lax.dot_general(a, b, _NT, preferred_element_type=F32)


def _rms(x, g):
    ms = jnp.mean(x * x, axis=-1, keepdims=True)
    return x * lax.rsqrt(ms + RMS_EPS) * g


def _sigmoid(x):
    return 1.0 / (1.0 + jnp.exp(-x))


def _silu(x):
    return x * _sigmoid(x)


def _log_sigmoid(x):
    return -(jnp.maximum(-x, 0.0) + jnp.log1p(jnp.exp(-jnp.abs(x))))


def _split3(x):
    hi = x.astype(BF16)
    r = x - hi.astype(F32)
    mid = r.astype(BF16)
    lo = (r - mid.astype(F32)).astype(BF16)
    return hi, mid, lo


def _lower_tri(n):
    row = lax.broadcasted_iota(jnp.int32, (n, n), 0)
    col = lax.broadcasted_iota(jnp.int32, (n, n), 1)
    return (col <= row).astype(BF16)


def _cumsum_rows(x):
    tri = _lower_tri(x.shape[0])
    a, b, c = _split3(x)
    return _dot(tri, a) + _dot(tri, b) + _dot(tri, c)


def _const_spec(shape):
    zeros = (0,) * len(shape)
    return pl.BlockSpec(shape, lambda *_: zeros, pipeline_mode=pl.Buffered(1))


def _memkv_kernel(mem_ref, g_ref, w_ref, kv_ref, kvb_ref):
    n = _rms(mem_ref[...], g_ref[...]).astype(BF16)
    kv = _dot(n, w_ref[...])
    kv_ref[...] = kv
    kvb_ref[...] = kv.astype(BF16)


def _mem_kv(mem, g_mem, w_kv_b):
    return pl.pallas_call(
        _memkv_kernel,
        out_shape=(jax.ShapeDtypeStruct((N_MEM, 2 * D_MODEL), F32),
                   jax.ShapeDtypeStruct((N_MEM, 2 * D_MODEL), BF16)),
        compiler_params=pltpu.CompilerParams(vmem_limit_bytes=VMEM_LIMIT),
        name="mem_kv",
    )(mem, g_mem, w_kv_b)


def _mem_attention(qm, mk_ref, mv_ref):
    parts = []
    for hh in range(H_M):
        sl = slice(hh * HD_M, (hh + 1) * HD_M)
        s = _dot_nt(qm[:, sl].astype(BF16), mk_ref[:, sl]) * MEM_SCALE
        s = s - jnp.max(s, axis=-1, keepdims=True)
        p = jnp.exp(s)
        p = p / jnp.sum(p, axis=-1, keepdims=True)
        parts.append(_dot(p.astype(BF16), mv_ref[:, sl]))
    return jnp.concatenate(parts, axis=1)


def _branch_cm_kernel(x_ref, g_ref, wc4_ref, wmem_ref, wga_ref, wgm_ref, wconv_ref, bconv_ref,
                      mk_ref, mv_ref, wupc_ref, wupm_ref, mcm_ref, cstate_ref, ubuf):
    i = pl.program_id(0)

    @pl.when(i == 0)
    def _():
        ubuf[0:8, :] = jnp.zeros((8, D_MODEL), F32)

    h = _rms(x_ref[...], g_ref[...]).astype(BF16)

    def seg(w_ref, k):
        return _dot(h, w_ref[:, k * D_MODEL:(k + 1) * D_MODEL])

    u = seg(wc4_ref, 2) * seg(wc4_ref, 0)
    ubuf[8:8 + TM, :] = u
    u1 = ubuf[7:7 + TM, :]
    u2 = ubuf[6:6 + TM, :]
    wc = wconv_ref[...]
    conv = wc[0:1, :] * u2 + wc[1:2, :] * u1 + wc[2:3, :] * u + bconv_ref[...]
    cstate_ref[...] = ubuf[TM + 6:TM + 8, :]
    ubuf[0:8, :] = ubuf[TM:TM + 8, :]
    o_c = seg(wc4_ref, 1) * conv * _silu(seg(wc4_ref, 3))
    m = _sigmoid(seg(wga_ref, 0)) * _dot(o_c.astype(BF16), wupc_ref[...])

    o_m = _mem_attention(seg(wmem_ref, 0), mk_ref, mv_ref) * _silu(seg(wmem_ref, 1))
    m = m + _sigmoid(seg(wgm_ref, 0)) * _dot(o_m.astype(BF16), wupm_ref[...])
    mcm_ref[...] = m


def _wcols(n_seg, first_seg):
    return pl.BlockSpec((D_MODEL, n_seg * D_MODEL), lambda *_: (0, first_seg // n_seg),
                        pipeline_mode=pl.Buffered(1))


def _branch_cm(x, g_pre, w_all, w_conv, b_conv, mk_b, mv_b, wupc, wupm):
    s = x.shape[0]
    row = lambda i: (i, 0)
    return pl.pallas_call(
        _branch_cm_kernel,
        grid=(s // TM,),
        in_specs=[
            pl.BlockSpec((TM, D_MODEL), row),
            _const_spec((1, D_MODEL)),
            _wcols(4, SEG_XC),
            _wcols(2, SEG_QM),
            _wcols(1, SEG_GA),
            _wcols(1, SEG_GM),
            _const_spec((3, D_MODEL)),
            _const_spec((1, D_MODEL)),
            _const_spec((N_MEM, D_MODEL)),
            _const_spec((N_MEM, D_MODEL)),
            _const_spec((D_MODEL, D_MODEL)),
            _const_spec((D_MODEL, D_MODEL)),
        ],
        out_specs=(pl.BlockSpec((TM, D_MODEL), row),
                   pl.BlockSpec((2, D_MODEL), lambda i: (0, 0))),
        out_shape=(jax.ShapeDtypeStruct((s, D_MODEL), F32),
                   jax.ShapeDtypeStruct((2, D_MODEL), F32)),
        scratch_shapes=[pltpu.VMEM((TM + 8, D_MODEL), F32)],
        compiler_params=pltpu.CompilerParams(
            dimension_semantics=("arbitrary",), vmem_limit_bytes=VMEM_LIMIT),
        name="prompt_branch_cm",
    )(x, g_pre, w_all, w_all, w_all, w_all, w_conv, b_conv, mk_b, mv_b, wupc, wupm)


def _decay_placement():
    eq = np.zeros((3, SLOT, H_F * SLOT), np.float32)
    ek = np.zeros((3, SLOT, H_F * SLOT), np.float32)
    for h in range(H_F):
        base = h * SLOT + HD_F
        for part in range(3):
            eq[part, h, base + part] = 1.0
            ek[part, h, base + 3 + part] = 1.0
        eq[0, H_F, base + 3:base + 6] = 1.0
        ek[0, H_F, base:base + 3] = 1.0
    return (jnp.asarray(eq.reshape(3 * SLOT, H_F * SLOT), BF16),
            jnp.asarray(ek.reshape(3 * SLOT, H_F * SLOT), BF16))


def _store_head_slots(out_ref, x, aug):
    low = lax.broadcasted_iota(jnp.int32, (x.shape[0], SLOT), 1) < HD_F
    for pair in range(H_F // 2):
        xp = x[:, pair * SLOT:(pair + 1) * SLOT]
        for half, cols in enumerate((xp, pltpu.roll(xp, HD_F, axis=1))):
            hh = 2 * pair + half
            out_ref[hh] = jnp.where(low, cols, aug[:, hh * SLOT:(hh + 1) * SLOT]).astype(BF16)


def _proj_fox_kernel(x_ref, g_ref, wq_ref, wk_ref, wv_ref, wvt_ref, vone_ref,
                     wzf_ref, wgf_ref, wf_ref, bf_ref, eq_ref, ek_ref,
                     k_ref, v_ref, qx_ref, kx_ref, vt_ref, zf_ref, sgf_ref,
                     logf_ref, carry):
    i = pl.program_id(0)

    @pl.when(i == 0)
    def _():
        carry[...] = jnp.zeros((1, SLOT), F32)

    h = _rms(x_ref[...], g_ref[...]).astype(BF16)
    k = _dot(h, wk_ref[...])
    k_ref[...] = k
    v_ref[...] = _dot(h, wv_ref[...])
    vt_ref[...] = (_dot_nt(wvt_ref[...], h) + vone_ref[...]).astype(BF16)
    zf_ref[...] = _dot(h, wzf_ref[...]).astype(BF16)
    sgf_ref[...] = _sigmoid(_dot(h, wgf_ref[...])).astype(BF16)

    lf = _log_sigmoid(_dot(h, wf_ref[...]) + bf_ref[...])
    logf_ref[...] = lf[:, 0:H_F]
    dcum = _cumsum_rows(lf) + carry[...]
    carry[...] = dcum[TM - 1:TM, :]
    dh, dm, dl = _split3(dcum * LOG2E)
    one_lane = lax.broadcasted_iota(jnp.int32, (TM, SLOT), 1) == H_F
    one = jnp.ones((TM, SLOT), BF16)

    q_aug = _dot(jnp.concatenate([jnp.where(one_lane, one, dh), dm, dl], axis=1), eq_ref[...])
    k_aug = _dot(jnp.concatenate([jnp.where(one_lane, one, -dh), -dm, -dl], axis=1), ek_ref[...])
    _store_head_slots(qx_ref, _dot(h, wq_ref[...]) * FOX_QSCALE, q_aug)
    _store_head_slots(kx_ref, k, k_aug)


def _proj_fox(x, g_pre, w_all, wvt, vone, bf, eq, ek):
    s = x.shape[0]
    row = lambda i: (i, 0)
    slot = lambda i: (0, i, 0)
    return pl.pallas_call(
        _proj_fox_kernel,
        grid=(s // TM,),
        in_specs=[
            pl.BlockSpec((TM, D_MODEL), row),
            _const_spec((1, D_MODEL)),
            _wcols(1, SEG_Q),
            _wcols(1, SEG_K),
            _wcols(1, SEG_V),
            _const_spec((H_F * VROWS, D_MODEL)),
            _const_spec((H_F * VROWS, 1)),
            _wcols(1, SEG_ZF),
            _wcols(1, SEG_GF),
            pl.BlockSpec((D_MODEL, SLOT), lambda i: (0, SEG_FORGET * D_MODEL // SLOT),
                         pipeline_mode=pl.Buffered(1)),
            _const_spec((1, SLOT)),
            _const_spec((3 * SLOT, H_F * SLOT)),
            _const_spec((3 * SLOT, H_F * SLOT)),
        ],
        out_specs=(
            pl.BlockSpec((TM, D_MODEL), row),
            pl.BlockSpec((TM, D_MODEL), row),
            pl.BlockSpec((H_F, TM, SLOT), slot),
            pl.BlockSpec((H_F, TM, SLOT), slot),
            pl.BlockSpec((H_F * VROWS, TM), lambda i: (0, i)),
            pl.BlockSpec((TM, D_MODEL), row),
            pl.BlockSpec((TM, D_MODEL), row),
            pl.BlockSpec((TM, H_F), row),
        ),
        out_shape=(
            jax.ShapeDtypeStruct((s, D_MODEL), F32),
            jax.ShapeDtypeStruct((s, D_MODEL), F32),
            jax.ShapeDtypeStruct((H_F, s, SLOT), BF16),
            jax.ShapeDtypeStruct((H_F, s, SLOT), BF16),
            jax.ShapeDtypeStruct((H_F * VROWS, s), BF16),
            jax.ShapeDtypeStruct((s, D_MODEL), BF16),
            jax.ShapeDtypeStruct((s, D_MODEL), BF16),
            jax.ShapeDtypeStruct((s, H_F), F32),
        ),
        scratch_shapes=[pltpu.VMEM((1, SLOT), F32)],
        compiler_params=pltpu.CompilerParams(
            dimension_semantics=("arbitrary",), vmem_limit_bytes=VMEM_LIMIT),
        name="prompt_proj_fox",
    )(x, g_pre, w_all, w_all, w_all, wvt, vone, w_all, w_all, w_all, bf, eq, ek)


def _attn_kernel(it_ref, jt_ref, qx_ref, kx_ref, vt_ref, o_ref, m_sc, acc_sc):
    step = pl.program_id(0)
    i = it_ref[step]
    j = jt_ref[step]

    @pl.when(j == 0)
    def _():
        m_sc[...] = jnp.full((H_F, BLK), -jnp.inf, F32)
        acc_sc[...] = jnp.zeros((H_F * VROWS, BLK), F32)

    def colmax(a):
        half = a.shape[0] // 2
        return jnp.maximum(jnp.max(a[:half], axis=0, keepdims=True),
                           jnp.max(a[half:], axis=0, keepdims=True))

    def logits(sub, hh):
        return _dot_nt(kx_ref[hh, sub * SUB:(sub + 1) * SUB, :], qx_ref[hh])

    def block(masked):
        units = [(sub, hh) for sub in range(BLK // SUB) for hh in range(H_F)]
        qry = lax.broadcasted_iota(jnp.int32, (SUB, BLK), 1)
        key = lax.broadcasted_iota(jnp.int32, (SUB, BLK), 0)
        pending = [logits(*u) for u in units[:LOOKAHEAD]]
        for n, (sub, hh) in enumerate(units):
            if n + LOOKAHEAD < len(units):
                pending.append(logits(*units[n + LOOKAHEAD]))
            st = pending.pop(0)
            if masked:
                st = jnp.where(key + sub * SUB <= qry, st, -jnp.inf)
            m_old = m_sc[hh:hh + 1, :]
            m_new = jnp.maximum(m_old, colmax(st))
            alpha = jnp.exp2(m_old - m_new)
            p = jnp.exp2(st - m_new)
            m_sc[hh:hh + 1, :] = m_new
            rows = slice(hh * VROWS, (hh + 1) * VROWS)
            pv = _dot(vt_ref[rows, sub * SUB:(sub + 1) * SUB], p.astype(BF16))
            acc_sc[rows, :] = alpha * acc_sc[rows, :] + pv

    @pl.when(j < i)
    def _():
        block(False)

    @pl.when(j == i)
    def _():
        block(True)
        heads = []
        for hh in range(H_F):
            base = hh * VROWS
            inv = 1.0 / acc_sc[base + HD_F:base + HD_F + 1, :]
            heads.append(acc_sc[base:base + HD_F, :] * inv)
        o_ref[...] = jnp.concatenate(heads, axis=0).T.astype(BF16)


def _attn_prompt(qx, kx, vt):
    s = qx.shape[1]
    nb = s // BLK
    it = np.concatenate([np.full(i + 1, i) for i in range(nb)]).astype(np.int32)
    jt = np.concatenate([np.arange(i + 1) for i in range(nb)]).astype(np.int32)
    grid_spec = pltpu.PrefetchScalarGridSpec(
        num_scalar_prefetch=2,
        grid=(it.shape[0],),
        in_specs=[
            pl.BlockSpec((H_F, BLK, SLOT), lambda t, it, jt: (0, it[t], 0)),
            pl.BlockSpec((H_F, BLK, SLOT), lambda t, it, jt: (0, jt[t], 0)),
            pl.BlockSpec((H_F * VROWS, BLK), lambda t, it, jt: (0, jt[t])),
        ],
        out_specs=pl.BlockSpec((BLK, D_MODEL), lambda t, it, jt: (it[t], 0)),
        scratch_shapes=[pltpu.VMEM((H_F, BLK), F32),
                        pltpu.VMEM((H_F * VROWS, BLK), F32)],
    )
    return pl.pallas_call(
        _attn_kernel,
        grid_spec=grid_spec,
        out_shape=jax.ShapeDtypeStruct((s, D_MODEL), BF16),
        compiler_params=pltpu.CompilerParams(
            dimension_semantics=("arbitrary",), vmem_limit_bytes=VMEM_LIMIT),
        name="prompt_fox_attention",
    )(jnp.asarray(it), jnp.asarray(jt), qx, kx, vt)


def _final_kernel(x_ref, mcm_ref, of_ref, zf_ref, sgf_ref, wupf_ref, wo_ref, g_ref, y_ref):
    og = (of_ref[...].astype(F32) * _silu(zf_ref[...].astype(F32))).astype(BF16)
    m = mcm_ref[...] + sgf_ref[...].astype(F32) * _dot(og, wupf_ref[...])
    y_ref[...] = x_ref[...] + _rms(_dot(m.astype(BF16), wo_ref[...]), g_ref[...])


def _final_prompt(x, mcm, of, zf, sgf, wupf, wo, g_post):
    s = x.shape[0]
    row = lambda i: (i, 0)
    tile = pl.BlockSpec((TM, D_MODEL), row)
    return pl.pallas_call(
        _final_kernel,
        grid=(s // TM,),
        in_specs=[tile, tile, tile, tile, tile,
                  _const_spec((D_MODEL, D_MODEL)),
                  _const_spec((D_MODEL, D_MODEL)),
                  _const_spec((1, D_MODEL))],
        out_specs=tile,
        out_shape=jax.ShapeDtypeStruct((s, D_MODEL), F32),
        compiler_params=pltpu.CompilerParams(
            dimension_semantics=("arbitrary",), vmem_limit_bytes=VMEM_LIMIT),
        name="prompt_merge_out",
    )(x, mcm, of, zf, sgf, wupf, wo, g_post)


def _proj_sample_kernel(x_ref, g_ref, w_ref, bf_ref, p_ref):
    j = pl.program_id(0)
    h = _rms(x_ref[...], g_ref[...]).astype(BF16)
    acc = _dot(h, w_ref[...])

    @pl.when(j != SEG_FORGET)
    def _():
        p_ref[...] = acc

    @pl.when(j == SEG_FORGET)
    def _():
        p_ref[...] = _log_sigmoid(acc + bf_ref[...])


def _proj_sample(x, g_pre, w_all, bf_pad):
    n = x.shape[0]
    return pl.pallas_call(
        _proj_sample_kernel,
        grid=(N_SEG,),
        in_specs=[pl.BlockSpec((n, D_MODEL), lambda j: (0, 0)),
                  pl.BlockSpec((1, D_MODEL), lambda j: (0, 0)),
                  pl.BlockSpec((D_MODEL, D_MODEL), lambda j: (0, j)),
                  pl.BlockSpec((1, D_MODEL), lambda j: (0, 0))],
        out_specs=pl.BlockSpec((n, D_MODEL), lambda j: (0, j)),
        out_shape=jax.ShapeDtypeStruct((n, N_SEG * D_MODEL), F32),
        compiler_params=pltpu.CompilerParams(
            dimension_semantics=("arbitrary",), vmem_limit_bytes=VMEM_LIMIT),
        name="sample_proj",
    )(x, g_pre, w_all, bf_pad)


T8 = 8


def _upper_tri(n):
    row = lax.broadcasted_iota(jnp.int32, (n, n), 0)
    col = lax.broadcasted_iota(jnp.int32, (n, n), 1)
    return (row <= col).astype(BF16)


def _attn_sample_kernel(pt_ref, q_ref, kn_ref, vn_ref, lfn_ref, *refs, n_new):
    pp = PAGES_PER_STEP
    k_refs = refs[0:pp]
    v_refs = refs[pp:2 * pp]
    lf_refs = refs[2 * pp:3 * pp]
    o_ref = refs[3 * pp]
    qbd, kbuf, vbuf, bias, m_sc, l_sc, acc_sc, carry = refs[3 * pp + 1:]
    rows = n_new * H_F
    c = pl.program_id(1)
    own = (lax.broadcasted_iota(jnp.int32, (H_F, D_MODEL), 1) // HD_F
           == lax.broadcasted_iota(jnp.int32, (H_F, D_MODEL), 0))

    @pl.when(c == 0)
    def _():
        q = q_ref[0] * FOX_QSCALE
        qbd[...] = jnp.concatenate(
            [jnp.where(own, q[t:t + 1, :], 0.0) for t in range(n_new)], axis=0).astype(BF16)
        m_sc[...] = jnp.full((rows, 1), -jnp.inf, F32)
        l_sc[...] = jnp.zeros((rows, 1), F32)
        acc_sc[...] = jnp.zeros((rows, D_MODEL), F32)
        carry[...] = jnp.zeros((H_F, 1), F32)

    tri = _upper_tri(PAGE)

    def decay_bias(lf_t, base):
        a, b, d = _split3(lf_t)
        dcum = _dot(a, tri) + _dot(b, tri) + _dot(d, tri) + base
        nb = -(dcum * LOG2E)
        return dcum[:, PAGE - 1:PAGE], jnp.concatenate([nb] * n_new, axis=0)

    def update(s, pv_fn):
        m_old = m_sc[...]
        m_new = jnp.maximum(m_old, jnp.max(s, axis=1, keepdims=True))
        alpha = jnp.exp2(m_old - m_new)
        p = jnp.exp2(s - m_new)
        l_sc[...] = alpha * l_sc[...] + jnp.sum(p, axis=1, keepdims=True)
        acc_sc[...] = alpha * acc_sc[...] + pv_fn(p.astype(BF16))
        m_sc[...] = m_new

    base = carry[...]
    for r in range(pp):
        cols = slice(r * PAGE, (r + 1) * PAGE)
        kbuf[:, cols] = k_refs[r][0].astype(BF16)
        vbuf[:, cols] = v_refs[r][0].astype(BF16)
        base, bias[:, cols] = decay_bias(lf_refs[r][0], base)
    carry[...] = base
    update(_dot(qbd[...], kbuf[...]) + bias[...], lambda p: _dot_nt(p, vbuf[...]))

    @pl.when(c == pl.num_programs(1) - 1)
    def _():
        pad = jnp.zeros((PAGE - T8, D_MODEL), F32)
        kn = jnp.concatenate([kn_ref[0], pad], axis=0).astype(BF16)
        vn = jnp.concatenate([vn_ref[0], pad], axis=0).astype(BF16)
        _, b_new = decay_bias(lfn_ref[0], carry[...])
        s = _dot_nt(qbd[...], kn) + b_new
        tok = lax.broadcasted_iota(jnp.int32, (rows, PAGE), 1)
        t_row = lax.broadcasted_iota(jnp.int32, (rows, PAGE), 0) // H_F
        s = jnp.where((tok < n_new) & (tok <= t_row), s, -jnp.inf)
        update(s, lambda p: _dot(p, vn))
        out = acc_sc[...] * (1.0 / l_sc[...])
        o_ref[0] = jnp.zeros((T8, D_MODEL), F32)
        for t in range(n_new):
            piece = jnp.where(own, out[t * H_F:(t + 1) * H_F, :], 0.0)
            o_ref[0, t:t + 1, :] = jnp.sum(piece, axis=0, keepdims=True)


def _attn_sample(page_table, q, kn8, vn8, lfn_t, cache_kt, cache_vt, cache_lft, n_new):
    nb, n_pages = page_table.shape
    pp = PAGES_PER_STEP
    rows = n_new * H_F
    per_b = lambda b, c, pt: (b, 0, 0)

    def page_map(r):
        return lambda b, c, pt: (pt[b * n_pages + c * pp + r], 0, 0)

    in_specs = [pl.BlockSpec((1, n_new, D_MODEL), per_b),
                pl.BlockSpec((1, T8, D_MODEL), per_b),
                pl.BlockSpec((1, T8, D_MODEL), per_b),
                pl.BlockSpec((1, H_F, PAGE), per_b)]
    in_specs += [pl.BlockSpec((1, D_MODEL, PAGE), page_map(r)) for r in range(pp)]
    in_specs += [pl.BlockSpec((1, D_MODEL, PAGE), page_map(r)) for r in range(pp)]
    in_specs += [pl.BlockSpec((1, H_F, PAGE), page_map(r)) for r in range(pp)]
    grid_spec = pltpu.PrefetchScalarGridSpec(
        num_scalar_prefetch=1,
        grid=(nb, n_pages // pp),
        in_specs=in_specs,
        out_specs=pl.BlockSpec((1, T8, D_MODEL), per_b),
        scratch_shapes=[
            pltpu.VMEM((rows, D_MODEL), BF16),
            pltpu.VMEM((D_MODEL, pp * PAGE), BF16),
            pltpu.VMEM((D_MODEL, pp * PAGE), BF16),
            pltpu.VMEM((rows, pp * PAGE), F32),
            pltpu.VMEM((rows, 1), F32),
            pltpu.VMEM((rows, 1), F32),
            pltpu.VMEM((rows, D_MODEL), F32),
            pltpu.VMEM((H_F, 1), F32),
        ],
    )
    args = [page_table.reshape(-1), q, kn8, vn8, lfn_t]
    args += [cache_kt] * pp + [cache_vt] * pp + [cache_lft] * pp
    return pl.pallas_call(
        functools.partial(_attn_sample_kernel, n_new=n_new),
        grid_spec=grid_spec,
        out_shape=jax.ShapeDtypeStruct((nb, T8, D_MODEL), F32),
        compiler_params=pltpu.CompilerParams(
            dimension_semantics=("arbitrary", "arbitrary"), vmem_limit_bytes=VMEM_LIMIT),
        name="sample_fox_attention",
    )(*args)


ROWS_M = H_M * T8


def _mem_sample_kernel(q_ref, k_ref, v_ref, o_ref):
    lane_head = lax.broadcasted_iota(jnp.int32, (T8, D_MODEL), 1) // HD_M
    q8 = q_ref[0]
    qbd = jnp.concatenate(
        [jnp.where(lane_head == hh, q8, 0.0) for hh in range(H_M)], axis=0).astype(BF16)
    s = _dot_nt(qbd, k_ref[0].astype(BF16)) * MEM_SCALE
    s = s - jnp.max(s, axis=-1, keepdims=True)
    p = jnp.exp(s)
    p = p / jnp.sum(p, axis=-1, keepdims=True)
    out = _dot(p.astype(BF16), v_ref[0].astype(BF16))
    o8 = jnp.zeros((T8, D_MODEL), F32)
    for hh in range(H_M):
        o8 = o8 + jnp.where(lane_head == hh, out[hh * T8:(hh + 1) * T8, :], 0.0)
    o_ref[0] = o8


def _mem_sample(qm8, mem_k, mem_v):
    nb = qm8.shape[0]
    per_b = lambda b: (b, 0, 0)
    return pl.pallas_call(
        _mem_sample_kernel,
        grid=(nb,),
        in_specs=[pl.BlockSpec((1, T8, D_MODEL), per_b),
                  pl.BlockSpec((1, N_MEM, D_MODEL), per_b),
                  pl.BlockSpec((1, N_MEM, D_MODEL), per_b)],
        out_specs=pl.BlockSpec((1, T8, D_MODEL), per_b),
        out_shape=jax.ShapeDtypeStruct((nb, T8, D_MODEL), F32),
        compiler_params=pltpu.CompilerParams(
            dimension_semantics=("arbitrary",), vmem_limit_bytes=VMEM_LIMIT),
        name="sample_mem_attention",
    )(qm8, mem_k, mem_v)


def _merge_sample_kernel(x_ref, p_ref, st0_ref, st1_ref, of_ref, om_ref, wconv_ref, bconv_ref,
                         wupc_ref, wupf_ref, wupm_ref, wo_ref, g_ref, y_ref, u_ref, ubuf,
                         *, n_new):
    n = x_ref.shape[0]

    def seg(k):
        return p_ref[:, k * D_MODEL:(k + 1) * D_MODEL]

    u = seg(2) * seg(0)
    u_ref[...] = u
    ubuf[0:8, :] = jnp.zeros((8, D_MODEL), F32)
    ubuf[8:8 + n, :] = u
    t = lax.broadcasted_iota(jnp.int32, (n, D_MODEL), 0) % n_new
    u1 = jnp.where(t == 0, st1_ref[...], ubuf[7:7 + n, :])
    u2 = jnp.where(t == 0, st0_ref[...], jnp.where(t == 1, st1_ref[...], ubuf[6:6 + n, :]))
    wc = wconv_ref[...]
    conv = wc[0:1, :] * u2 + wc[1:2, :] * u1 + wc[2:3, :] * u + bconv_ref[...]
    o_c = seg(1) * conv * _silu(seg(3))
    o_f = of_ref[...] * _silu(seg(7))
    o_m = om_ref[...] * _silu(seg(9))
    m = (_sigmoid(seg(10)) * _dot(o_c.astype(BF16), wupc_ref[...])
         + _sigmoid(seg(11)) * _dot(o_f.astype(BF16), wupf_ref[...])
         + _sigmoid(seg(12)) * _dot(o_m.astype(BF16), wupm_ref[...]))
    y_ref[...] = x_ref[...] + _rms(_dot(m.astype(BF16), wo_ref[...]), g_ref[...])


def _merge_sample(x, ps, st0, st1, of, om, w_conv, b_conv, wupc, wupf, wupm, wo, g_post, n_new):
    n = x.shape[0]
    return pl.pallas_call(
        functools.partial(_merge_sample_kernel, n_new=n_new),
        out_shape=(jax.ShapeDtypeStruct((n, D_MODEL), F32),
                   jax.ShapeDtypeStruct((n, D_MODEL), F32)),
        scratch_shapes=[pltpu.VMEM((n + 8, D_MODEL), F32)],
        compiler_params=pltpu.CompilerParams(vmem_limit_bytes=VMEM_LIMIT),
        name="sample_merge_out",
    )(x, ps, st0, st1, of, om, w_conv, b_conv, wupc, wupf, wupm, wo, g_post)


def kernel(x_prompt, x_sample, cache_fox_k, cache_fox_v, cache_fox_logf, state_conv,
           cache_mem_k, cache_mem_v, page_table, mem_prompt,
           g_pre, w_in, b_f, w_conv, b_conv, g_mem, w_mem_kv,
           w_up_conv, w_up_fox, w_up_mem, w_o, g_post):
    bsz, seq, _ = x_prompt.shape
    assert bsz == 1 and seq % BLK == 0 and BLK % TM == 0
    nb, n_new, _ = x_sample.shape
    assert n_new <= T8 and page_table.shape[1] % PAGES_PER_STEP == 0
    assert cache_fox_k.shape[1] == PAGE

    o = D_MODEL
    n_front = SEG_QM * o
    w_all = jnp.concatenate(
        [w_in[:, :n_front], w_in[:, n_front + H_F:],
         jnp.pad(w_in[:, n_front:n_front + H_F], ((0, 0), (0, o - H_F)))], axis=1).astype(BF16)
    w_v = w_all[:, SEG_V * o:(SEG_V + 1) * o]
    b_f_slot = jnp.pad(b_f, (0, SLOT - H_F)).reshape(1, SLOT)
    b_f_seg = jnp.pad(b_f, (0, o - H_F)).reshape(1, o)
    wupc, wupf, wupm, wo = (w.astype(BF16) for w in (w_up_conv, w_up_fox, w_up_mem, w_o))
    g_pre2, g_mem2, g_post2 = (g.reshape(1, o) for g in (g_pre, g_mem, g_post))
    b_conv2 = b_conv.reshape(1, o)
    eq, ek = _decay_placement()
    w_vt = jnp.pad(w_v.T.reshape(H_F, HD_F, o), ((0, 0), (0, VROWS - HD_F), (0, 0)))
    w_vt = w_vt.reshape(H_F * VROWS, o)
    v_one = jnp.asarray((np.arange(H_F * VROWS) % VROWS == HD_F).astype(np.float32)[:, None])

    xp = x_prompt.reshape(seq, o)
    mkv, mkv_b = _mem_kv(mem_prompt.reshape(N_MEM, o), g_mem2, w_mem_kv.astype(BF16))
    mcm, cstate = _branch_cm(xp, g_pre2, w_all, w_conv, b_conv2,
                             mkv_b[:, :o], mkv_b[:, o:], wupc, wupm)
    k_p, v_p, qx, kx, vt, zf, sgf, logf_p = _proj_fox(
        xp, g_pre2, w_all, w_vt, v_one, b_f_slot, eq, ek)
    o_f = _attn_prompt(qx, kx, vt)
    y_p = _final_prompt(xp, mcm, o_f, zf, sgf, wupf, wo, g_post2)

    xs = x_sample.reshape(nb * n_new, o)
    ps = _proj_sample(xs, g_pre2, w_all, b_f_seg)

    def seg(k):
        return ps[:, k * o:(k + 1) * o]

    def pad8(a):
        a = a.reshape(nb, n_new, a.shape[-1])
        return jnp.pad(a, ((0, 0), (0, T8 - n_new), (0, 0)))

    k_s, v_s = seg(5), seg(6)
    logf_s = ps[:, SEG_FORGET * o:SEG_FORGET * o + H_F]
    n_phys = cache_fox_k.shape[0]
    cache_kt = jnp.transpose(cache_fox_k, (0, 2, 3, 1)).reshape(n_phys, o, PAGE)
    cache_vt = jnp.transpose(cache_fox_v, (0, 2, 3, 1)).reshape(n_phys, o, PAGE)
    cache_lft = jnp.transpose(cache_fox_logf, (0, 2, 1))
    lfn_t = jnp.pad(jnp.transpose(logf_s.reshape(nb, n_new, H_F), (0, 2, 1)),
                    ((0, 0), (0, 0), (0, PAGE - n_new)))
    of8 = _attn_sample(page_table, seg(4).reshape(nb, n_new, o), pad8(k_s), pad8(v_s), lfn_t,
                       cache_kt, cache_vt, cache_lft, n_new)
    om8 = _mem_sample(pad8(seg(8)), cache_mem_k.reshape(nb, N_MEM, o),
                      cache_mem_v.reshape(nb, N_MEM, o))
    st0 = jnp.repeat(state_conv[:, 0, :], n_new, axis=0)
    st1 = jnp.repeat(state_conv[:, 1, :], n_new, axis=0)
    y_s, u_s = _merge_sample(xs, ps, st0, st1,
                             of8[:, :n_new].reshape(nb * n_new, o),
                             om8[:, :n_new].reshape(nb * n_new, o),
                             w_conv, b_conv2, wupc, wupf, wupm, wo, g_post2, n_new)

    return (y_p.reshape(1, seq, o),
            y_s.reshape(nb, n_new, o),
            k_p.reshape(1, seq, H_F, HD_F),
            v_p.reshape(1, seq, H_F, HD_F),
            logf_p.reshape(1, seq, H_F),
            cstate.reshape(1, 2, o),
            mkv[:, :o].reshape(1, N_MEM, H_M, HD_M),
            mkv[:, o:].reshape(1, N_MEM, H_M, HD_M),
            k_s.reshape(nb, n_new, H_F, HD_F),
            v_s.reshape(nb, n_new, H_F, HD_F),
            logf_s.reshape(nb, n_new, H_F),
            u_s.reshape(nb, n_new, o)[:, n_new - 2:, :])
```

```python
import functools

import numpy as np
import jax
import jax.numpy as jnp
from jax import lax
from jax.experimental import pallas as pl
from jax.experimental.pallas import tpu as pltpu

D_MODEL = 1024
H_F = 16
HD_F = 64
H_M = 4
HD_M = 256
N_MEM = 256
PAGE = 128
RMS_EPS = 1e-6
LOG2E = 1.4426950408889634
FOX_QSCALE = (HD_F ** -0.5) * LOG2E
MEM_SCALE = HD_M ** -0.5

F32 = jnp.float32
BF16 = jnp.bfloat16

TM = 256
BLK = 512
SUB = 256
LOOKAHEAD = 2
SLOT = 128
VROWS = 80

(SEG_XC, SEG_BC, SEG_CC, SEG_ZC, SEG_Q, SEG_K, SEG_V, SEG_ZF,
 SEG_QM, SEG_ZM, SEG_GA, SEG_GF, SEG_GM, SEG_FORGET) = range(14)
N_SEG = 14

VMEM_LIMIT = 56 * 1024 * 1024

_NT = (((1,), (1,)), ((), ()))


def _dot(a, b):
    return jnp.dot(a, b, preferred_element_type=F32)


def _dot_nt(a, b):
    return lax.dot_general(a, b, _NT, preferred_element_type=F32)


def _rms(x, g):
    ms = jnp.mean(x * x, axis=-1, keepdims=True)
    return x * lax.rsqrt(ms + RMS_EPS) * g


def _sigmoid(x):
    return 1.0 / (1.0 + jnp.exp(-x))


def _silu(x):
    return x * _sigmoid(x)


def _log_sigmoid(x):
    return -(jnp.maximum(-x, 0.0) + jnp.log1p(jnp.exp(-jnp.abs(x))))


def _split3(x):
    hi = x.astype(BF16)
    r = x - hi.astype(F32)
    mid = r.astype(BF16)
    lo = (r - mid.astype(F32)).astype(BF16)
    return hi, mid, lo


def _lower_tri(n):
    row = lax.broadcasted_iota(jnp.int32, (n, n), 0)
    col = lax.broadcasted_iota(jnp.int32, (n, n), 1)
    return (col <= row).astype(BF16)


def _cumsum_rows(x):
    tri = _lower_tri(x.shape[0])
    a, b, c = _split3(x)
    return _dot(tri, a) + _dot(tri, b) + _dot(tri, c)


def _const_spec(shape):
    zeros = (0,) * len(shape)
    return pl.BlockSpec(shape, lambda *_: zeros, pipeline_mode=pl.Buffered(1))


def _memkv_kernel(mem_ref, g_ref, w_ref, kv_ref, kvb_ref):
    n = _rms(mem_ref[...], g_ref[...]).astype(BF16)
    kv = _dot(n, w_ref[...])
    kv_ref[...] = kv
    kvb_ref[...] = kv.astype(BF16)


def _mem_kv(mem, g_mem, w_kv_b):
    return pl.pallas_call(
        _memkv_kernel,
        out_shape=(jax.ShapeDtypeStruct((N_MEM, 2 * D_MODEL), F32),
                   jax.ShapeDtypeStruct((N_MEM, 2 * D_MODEL), BF16)),
        compiler_params=pltpu.CompilerParams(vmem_limit_bytes=VMEM_LIMIT),
        name="mem_kv",
    )(mem, g_mem, w_kv_b)


def _mem_attention(qm, mk_ref, mv_ref):
    parts = []
    for hh in range(H_M):
        sl = slice(hh * HD_M, (hh + 1) * HD_M)
        s = _dot_nt(qm[:, sl].astype(BF16), mk_ref[:, sl]) * MEM_SCALE
        s = s - jnp.max(s, axis=-1, keepdims=True)
        p = jnp.exp(s)
        p = p / jnp.sum(p, axis=-1, keepdims=True)
        parts.append(_dot(p.astype(BF16), mv_ref[:, sl]))
    return jnp.concatenate(parts, axis=1)


def _branch_cm_kernel(x_ref, g_ref, wc4_ref, wmem_ref, wga_ref, wgm_ref, wconv_ref, bconv_ref,
                      mk_ref, mv_ref, wupc_ref, wupm_ref, mcm_ref, cstate_ref, ubuf):
    i = pl.program_id(0)

    @pl.when(i == 0)
    def _():
        ubuf[0:8, :] = jnp.zeros((8, D_MODEL), F32)

    h = _rms(x_ref[...], g_ref[...]).astype(BF16)

    def seg(w_ref, k):
        return _dot(h, w_ref[:, k * D_MODEL:(k + 1) * D_MODEL])

    u = seg(wc4_ref, 2) * seg(wc4_ref, 0)
    ubuf[8:8 + TM, :] = u
    u1 = ubuf[7:7 + TM, :]
    u2 = ubuf[6:6 + TM, :]
    wc = wconv_ref[...]
    conv = wc[0:1, :] * u2 + wc[1:2, :] * u1 + wc[2:3, :] * u + bconv_ref[...]
    cstate_ref[...] = ubuf[TM + 6:TM + 8, :]
    ubuf[0:8, :] = ubuf[TM:TM + 8, :]
    o_c = seg(wc4_ref, 1) * conv * _silu(seg(wc4_ref, 3))
    m = _sigmoid(seg(wga_ref, 0)) * _dot(o_c.astype(BF16), wupc_ref[...])

    o_m = _mem_attention(seg(wmem_ref, 0), mk_ref, mv_ref) * _silu(seg(wmem_ref, 1))
    m = m + _sigmoid(seg(wgm_ref, 0)) * _dot(o_m.astype(BF16), wupm_ref[...])
    mcm_ref[...] = m


def _wcols(n_seg, first_seg):
    return pl.BlockSpec((D_MODEL, n_seg * D_MODEL), lambda *_: (0, first_seg // n_seg),
                        pipeline_mode=pl.Buffered(1))


def _branch_cm(x, g_pre, w_all, w_conv, b_conv, mk_b, mv_b, wupc, wupm):
    s = x.shape[0]
    row = lambda i: (i, 0)
    return pl.pallas_call(
        _branch_cm_kernel,
        grid=(s // TM,),
        in_specs=[
            pl.BlockSpec((TM, D_MODEL), row),
            _const_spec((1, D_MODEL)),
            _wcols(4, SEG_XC),
            _wcols(2, SEG_QM),
            _wcols(1, SEG_GA),
            _wcols(1, SEG_GM),
            _const_spec((3, D_MODEL)),
            _const_spec((1, D_MODEL)),
            _const_spec((N_MEM, D_MODEL)),
            _const_spec((N_MEM, D_MODEL)),
            _const_spec((D_MODEL, D_MODEL)),
            _const_spec((D_MODEL, D_MODEL)),
        ],
        out_specs=(pl.BlockSpec((TM, D_MODEL), row),
                   pl.BlockSpec((2, D_MODEL), lambda i: (0, 0))),
        out_shape=(jax.ShapeDtypeStruct((s, D_MODEL), F32),
                   jax.ShapeDtypeStruct((2, D_MODEL), F32)),
        scratch_shapes=[pltpu.VMEM((TM + 8, D_MODEL), F32)],
        compiler_params=pltpu.CompilerParams(
            dimension_semantics=("arbitrary",), vmem_limit_bytes=VMEM_LIMIT),
        name="prompt_branch_cm",
    )(x, g_pre, w_all, w_all, w_all, w_all, w_conv, b_conv, mk_b, mv_b, wupc, wupm)


def _decay_placement():
    eq = np.zeros((3, SLOT, H_F * SLOT), np.float32)
    ek = np.zeros((3, SLOT, H_F * SLOT), np.float32)
    for h in range(H_F):
        base = h * SLOT + HD_F
        for part in range(3):
            eq[part, h, base + part] = 1.0
            ek[part, h, base + 3 + part] = 1.0
        eq[0, H_F, base + 3:base + 6] = 1.0
        ek[0, H_F, base:base + 3] = 1.0
    return (jnp.asarray(eq.reshape(3 * SLOT, H_F * SLOT), BF16),
            jnp.asarray(ek.reshape(3 * SLOT, H_F * SLOT), BF16))


def _store_head_slots(out_ref, x, aug):
    low = lax.broadcasted_iota(jnp.int32, (x.shape[0], SLOT), 1) < HD_F
    for pair in range(H_F // 2):
        xp = x[:, pair * SLOT:(pair + 1) * SLOT]
        for half, cols in enumerate((xp, pltpu.roll(xp, HD_F, axis=1))):
            hh = 2 * pair + half
            out_ref[hh] = jnp.where(low, cols, aug[:, hh * SLOT:(hh + 1) * SLOT]).astype(BF16)


def _proj_fox_kernel(x_ref, g_ref, wq_ref, wk_ref, wv_ref, wvt_ref, vone_ref,
                     wzf_ref, wgf_ref, wf_ref, bf_ref, eq_ref, ek_ref,
                     k_ref, v_ref, qx_ref, kx_ref, vt_ref, zf_ref, sgf_ref,
                     logf_ref, carry):
    i = pl.program_id(0)

    @pl.when(i == 0)
    def _():
        carry[...] = jnp.zeros((1, SLOT), F32)

    h = _rms(x_ref[...], g_ref[...]).astype(BF16)
    k = _dot(h, wk_ref[...])
    k_ref[...] = k
    v_ref[...] = _dot(h, wv_ref[...])
    vt_ref[...] = (_dot_nt(wvt_ref[...], h) + vone_ref[...]).astype(BF16)
    zf_ref[...] = _dot(h, wzf_ref[...]).astype(BF16)
    sgf_ref[...] = _sigmoid(_dot(h, wgf_ref[...])).astype(BF16)

    lf = _log_sigmoid(_dot(h, wf_ref[...]) + bf_ref[...])
    logf_ref[...] = lf[:, 0:H_F]
    dcum = _cumsum_rows(lf) + carry[...]
    carry[...] = dcum[TM - 1:TM, :]
    dh, dm, dl = _split3(dcum * LOG2E)
    one_lane = lax.broadcasted_iota(jnp.int32, (TM, SLOT), 1) == H_F
    one = jnp.ones((TM, SLOT), BF16)

    q_aug = _dot(jnp.concatenate([jnp.where(one_lane, one, dh), dm, dl], axis=1), eq_ref[...])
    k_aug = _dot(jnp.concatenate([jnp.where(one_lane, one, -dh), -dm, -dl], axis=1), ek_ref[...])
    _store_head_slots(qx_ref, _dot(h, wq_ref[...]) * FOX_QSCALE, q_aug)
    _store_head_slots(kx_ref, k, k_aug)


def _proj_fox(x, g_pre, w_all, wvt, vone, bf, eq, ek):
    s = x.shape[0]
    row = lambda i: (i, 0)
    slot = lambda i: (0, i, 0)
    return pl.pallas_call(
        _proj_fox_kernel,
        grid=(s // TM,),
        in_specs=[
            pl.BlockSpec((TM, D_MODEL), row),
            _const_spec((1, D_MODEL)),
            _wcols(1, SEG_Q),
            _wcols(1, SEG_K),
            _wcols(1, SEG_V),
            _const_spec((H_F * VROWS, D_MODEL)),
            _const_spec((H_F * VROWS, 1)),
            _wcols(1, SEG_ZF),
            _wcols(1, SEG_GF),
            pl.BlockSpec((D_MODEL, SLOT), lambda i: (0, SEG_FORGET * D_MODEL // SLOT),
                         pipeline_mode=pl.Buffered(1)),
            _const_spec((1, SLOT)),
            _const_spec((3 * SLOT, H_F * SLOT)),
            _const_spec((3 * SLOT, H_F * SLOT)),
        ],
        out_specs=(
            pl.BlockSpec((TM, D_MODEL), row),
            pl.BlockSpec((TM, D_MODEL), row),
            pl.BlockSpec((H_F, TM, SLOT), slot),
            pl.BlockSpec((H_F, TM, SLOT), slot),
            pl.BlockSpec((H_F * VROWS, TM), lambda i: (0, i)),
            pl.BlockSpec((TM, D_MODEL), row),
            pl.BlockSpec((TM, D_MODEL), row),
            pl.BlockSpec((TM, H_F), row),
        ),
        out_shape=(
            jax.ShapeDtypeStruct((s, D_MODEL), F32),
            jax.ShapeDtypeStruct((s, D_MODEL), F32),
            jax.ShapeDtypeStruct((H_F, s, SLOT), BF16),
            jax.ShapeDtypeStruct((H_F, s, SLOT), BF16),
            jax.ShapeDtypeStruct((H_F * VROWS, s), BF16),
            jax.ShapeDtypeStruct((s, D_MODEL), BF16),
            jax.ShapeDtypeStruct((s, D_MODEL), BF16),
            jax.ShapeDtypeStruct((s, H_F), F32),
        ),
        scratch_shapes=[pltpu.VMEM((1, SLOT), F32)],
        compiler_params=pltpu.CompilerParams(
            dimension_semantics=("arbitrary",), vmem_limit_bytes=VMEM_LIMIT),
        name="prompt_proj_fox",
    )(x, g_pre, w_all, w_all, w_all, wvt, vone, w_all, w_all, w_all, bf, eq, ek)


T8 = 8


def _upper_tri(n):
    row = lax.broadcasted_iota(jnp.int32, (n, n), 0)
    col = lax.broadcasted_iota(jnp.int32, (n, n), 1)
    return (row <= col).astype(BF16)


def _attn_kernel(it_ref, jt_ref, pt_ref, qx_ref, kx_ref, vt_ref,
                 sq_ref, skn_ref, svn_ref, slfn_ref, *refs, n_new, pp, chunks, sample_steps):
    k_refs = refs[0:pp]
    v_refs = refs[pp:2 * pp]
    lf_refs = refs[2 * pp:3 * pp]
    o_ref, so_ref = refs[3 * pp:3 * pp + 2]
    m_sc, acc_sc, kbuf, vbuf, bias, sm_sc, sl_sc, sacc_sc, carry = refs[3 * pp + 2:]
    step = pl.program_id(0)
    i = it_ref[step]
    j = jt_ref[step]
    chunk = lax.rem(step, chunks)
    rows = n_new * H_F
    own = (lax.broadcasted_iota(jnp.int32, (H_F, D_MODEL), 1) // HD_F
           == lax.broadcasted_iota(jnp.int32, (H_F, D_MODEL), 0))

    @pl.when(j == 0)
    def _():
        m_sc[...] = jnp.full((H_F, BLK), -jnp.inf, F32)
        acc_sc[...] = jnp.zeros((H_F * VROWS, BLK), F32)

    @pl.when(step == 0)
    def _():
        sm_sc[...] = jnp.full((rows, 1), -jnp.inf, F32)
        sl_sc[...] = jnp.zeros((rows, 1), F32)
        sacc_sc[...] = jnp.zeros((rows, D_MODEL), F32)
        carry[...] = jnp.zeros((H_F, 1), F32)

    def block_diag_q():
        q = sq_ref[0] * FOX_QSCALE
        return jnp.concatenate(
            [jnp.where(own, q[t:t + 1, :], 0.0) for t in range(n_new)], axis=0).astype(BF16)

    def decay_bias(lf_t, base):
        tri = _upper_tri(PAGE)
        a, b, d = _split3(lf_t)
        dcum = _dot(a, tri) + _dot(b, tri) + _dot(d, tri) + base
        nb = -(dcum * LOG2E)
        return dcum[:, PAGE - 1:PAGE], jnp.concatenate([nb] * n_new, axis=0)

    def sample_update(s, pv_fn, fresh):
        m_old, l_old, acc_old = sm_sc[...], sl_sc[...], sacc_sc[...]
        if fresh is not None:
            m_old = jnp.where(fresh, -jnp.inf, m_old)
            l_old = jnp.where(fresh, 0.0, l_old)
            acc_old = jnp.where(fresh, 0.0, acc_old)
        m_new = jnp.maximum(m_old, jnp.max(s, axis=1, keepdims=True))
        alpha = jnp.exp2(m_old - m_new)
        p = jnp.exp2(s - m_new)
        sl_sc[...] = alpha * l_old + jnp.sum(p, axis=1, keepdims=True)
        sacc_sc[...] = alpha * acc_old + pv_fn(p.astype(BF16))
        sm_sc[...] = m_new

    def sample_logits():
        base = jnp.where(chunk == 0, 0.0, carry[...])
        for r in range(pp):
            cols = slice(r * PAGE, (r + 1) * PAGE)
            kbuf[:, cols] = k_refs[r][0].astype(BF16)
            vbuf[:, cols] = v_refs[r][0].astype(BF16)
            base, bias[:, cols] = decay_bias(lf_refs[r][0], base)
        carry[...] = base
        return _dot(block_diag_q(), kbuf[...]) + bias[...]

    def colmax(a):
        half = a.shape[0] // 2
        return jnp.maximum(jnp.max(a[:half], axis=0, keepdims=True),
                           jnp.max(a[half:], axis=0, keepdims=True))

    def logits(sub, hh):
        return _dot_nt(kx_ref[hh, sub * SUB:(sub + 1) * SUB, :], qx_ref[hh])

    def block(masked):
        units = [(sub, hh) for sub in range(BLK // SUB) for hh in range(H_F)]
        qry = lax.broadcasted_iota(jnp.int32, (SUB, BLK), 1)
        key = lax.broadcasted_iota(jnp.int32, (SUB, BLK), 0)
        s_smp = sample_logits()
        pending = [logits(*u) for u in units[:LOOKAHEAD]]
        for n, (sub, hh) in enumerate(units):
            if n == len(units) // 2:
                sample_update(s_smp, lambda p: _dot_nt(p, vbuf[...]), chunk == 0)
            if n + LOOKAHEAD < len(units):
                pending.append(logits(*units[n + LOOKAHEAD]))
            st = pending.pop(0)
            if masked:
                st = jnp.where(key + sub * SUB <= qry, st, -jnp.inf)
            m_old = m_sc[hh:hh + 1, :]
            m_new = jnp.maximum(m_old, colmax(st))
            alpha = jnp.exp2(m_old - m_new)
            p = jnp.exp2(st - m_new)
            m_sc[hh:hh + 1, :] = m_new
            rows = slice(hh * VROWS, (hh + 1) * VROWS)
            pv = _dot(vt_ref[rows, sub * SUB:(sub + 1) * SUB], p.astype(BF16))
            acc_sc[rows, :] = alpha * acc_sc[rows, :] + pv

    @pl.when(j < i)
    def _():
        block(False)

    @pl.when(j == i)
    def _():
        block(True)
        heads = []
        for hh in range(H_F):
            base = hh * VROWS
            inv = 1.0 / acc_sc[base + HD_F:base + HD_F + 1, :]
            heads.append(acc_sc[base:base + HD_F, :] * inv)
        o_ref[...] = jnp.concatenate(heads, axis=0).T.astype(BF16)

    @pl.when((chunk == chunks - 1) & (step < sample_steps))
    def _():
        pad = jnp.zeros((PAGE - T8, D_MODEL), F32)
        kn = jnp.concatenate([skn_ref[0], pad], axis=0).astype(BF16)
        vn = jnp.concatenate([svn_ref[0], pad], axis=0).astype(BF16)
        _, b_new = decay_bias(slfn_ref[0], carry[...])
        s = _dot_nt(block_diag_q(), kn) + b_new
        tok = lax.broadcasted_iota(jnp.int32, (rows, PAGE), 1)
        t_row = lax.broadcasted_iota(jnp.int32, (rows, PAGE), 0) // H_F
        s = jnp.where((tok < n_new) & (tok <= t_row), s, -jnp.inf)
        sample_update(s, lambda p: _dot(p, vn), None)
        out = sacc_sc[...] * (1.0 / sl_sc[...])
        so_ref[0] = jnp.zeros((T8, D_MODEL), F32)
        for t in range(n_new):
            piece = jnp.where(own, out[t * H_F:(t + 1) * H_F, :], 0.0)
            so_ref[0, t:t + 1, :] = jnp.sum(piece, axis=0, keepdims=True)


def _attn_both(qx, kx, vt, page_table, q, kn8, vn8, lfn_t, cache_kt, cache_vt, cache_lft, n_new):
    s = qx.shape[1]
    nblk = s // BLK
    it = np.concatenate([np.full(i + 1, i) for i in range(nblk)]).astype(np.int32)
    jt = np.concatenate([np.arange(i + 1) for i in range(nblk)]).astype(np.int32)
    n_steps = it.shape[0]
    nb, n_pages = page_table.shape
    pp = min(d for d in range(1, n_pages + 1)
             if n_pages % d == 0 and nb * (n_pages // d) <= n_steps)
    chunks = n_pages // pp
    sample_steps = nb * chunks
    rows = n_new * H_F

    def per_seq(t, it, jt, pt):
        return (jnp.minimum(t // chunks, nb - 1), 0, 0)

    def page_map(r):
        return lambda t, it, jt, pt: (pt[jnp.minimum(t, sample_steps - 1) * pp + r], 0, 0)

    in_specs = [
        pl.BlockSpec((H_F, BLK, SLOT), lambda t, it, jt, pt: (0, it[t], 0)),
        pl.BlockSpec((H_F, BLK, SLOT), lambda t, it, jt, pt: (0, jt[t], 0)),
        pl.BlockSpec((H_F * VROWS, BLK), lambda t, it, jt, pt: (0, jt[t])),
        pl.BlockSpec((1, n_new, D_MODEL), per_seq),
        pl.BlockSpec((1, T8, D_MODEL), per_seq),
        pl.BlockSpec((1, T8, D_MODEL), per_seq),
        pl.BlockSpec((1, H_F, PAGE), per_seq),
    ]
    in_specs += [pl.BlockSpec((1, D_MODEL, PAGE), page_map(r)) for r in range(pp)]
    in_specs += [pl.BlockSpec((1, D_MODEL, PAGE), page_map(r)) for r in range(pp)]
    in_specs += [pl.BlockSpec((1, H_F, PAGE), page_map(r)) for r in range(pp)]
    grid_spec = pltpu.PrefetchScalarGridSpec(
        num_scalar_prefetch=3,
        grid=(n_steps,),
        in_specs=in_specs,
        out_specs=(pl.BlockSpec((BLK, D_MODEL), lambda t, it, jt, pt: (it[t], 0)),
                   pl.BlockSpec((1, T8, D_MODEL), per_seq)),
        scratch_shapes=[
            pltpu.VMEM((H_F, BLK), F32),
            pltpu.VMEM((H_F * VROWS, BLK), F32),
            pltpu.VMEM((D_MODEL, pp * PAGE), BF16),
            pltpu.VMEM((D_MODEL, pp * PAGE), BF16),
            pltpu.VMEM((rows, pp * PAGE), F32),
            pltpu.VMEM((rows, 1), F32),
            pltpu.VMEM((rows, 1), F32),
            pltpu.VMEM((rows, D_MODEL), F32),
            pltpu.VMEM((H_F, 1), F32),
        ],
    )
    args = [jnp.asarray(it), jnp.asarray(jt), page_table.reshape(-1), qx, kx, vt, q, kn8, vn8, lfn_t]
    args += [cache_kt] * pp + [cache_vt] * pp + [cache_lft] * pp
    return pl.pallas_call(
        functools.partial(_attn_kernel, n_new=n_new, pp=pp, chunks=chunks,
                          sample_steps=sample_steps),
        grid_spec=grid_spec,
        out_shape=(jax.ShapeDtypeStruct((s, D_MODEL), BF16),
                   jax.ShapeDtypeStruct((nb, T8, D_MODEL), F32)),
        compiler_params=pltpu.CompilerParams(
            dimension_semantics=("arbitrary",), vmem_limit_bytes=VMEM_LIMIT),
        name="fox_attention",
    )(*args)


def _final_kernel(x_ref, mcm_ref, of_ref, zf_ref, sgf_ref, wupf_ref, wo_ref, g_ref, y_ref):
    og = (of_ref[...].astype(F32) * _silu(zf_ref[...].astype(F32))).astype(BF16)
    m = mcm_ref[...] + sgf_ref[...].astype(F32) * _dot(og, wupf_ref[...])
    y_ref[...] = x_ref[...] + _rms(_dot(m.astype(BF16), wo_ref[...]), g_ref[...])


def _final_prompt(x, mcm, of, zf, sgf, wupf, wo, g_post):
    s = x.shape[0]
    row = lambda i: (i, 0)
    tile = pl.BlockSpec((TM, D_MODEL), row)
    return pl.pallas_call(
        _final_kernel,
        grid=(s // TM,),
        in_specs=[tile, tile, tile, tile, tile,
                  _const_spec((D_MODEL, D_MODEL)),
                  _const_spec((D_MODEL, D_MODEL)),
                  _const_spec((1, D_MODEL))],
        out_specs=tile,
        out_shape=jax.ShapeDtypeStruct((s, D_MODEL), F32),
        compiler_params=pltpu.CompilerParams(
            dimension_semantics=("arbitrary",), vmem_limit_bytes=VMEM_LIMIT),
        name="prompt_merge_out",
    )(x, mcm, of, zf, sgf, wupf, wo, g_post)


def _proj_sample_kernel(x_ref, g_ref, w_ref, bf_ref, p_ref):
    j = pl.program_id(0)
    h = _rms(x_ref[...], g_ref[...]).astype(BF16)
    acc = _dot(h, w_ref[...])

    @pl.when(j != SEG_FORGET)
    def _():
        p_ref[...] = acc

    @pl.when(j == SEG_FORGET)
    def _():
        p_ref[...] = _log_sigmoid(acc + bf_ref[...])


def _proj_sample(x, g_pre, w_all, bf_pad):
    n = x.shape[0]
    return pl.pallas_call(
        _proj_sample_kernel,
        grid=(N_SEG,),
        in_specs=[pl.BlockSpec((n, D_MODEL), lambda j: (0, 0)),
                  pl.BlockSpec((1, D_MODEL), lambda j: (0, 0)),
                  pl.BlockSpec((D_MODEL, D_MODEL), lambda j: (0, j)),
                  pl.BlockSpec((1, D_MODEL), lambda j: (0, 0))],
        out_specs=pl.BlockSpec((n, D_MODEL), lambda j: (0, j)),
        out_shape=jax.ShapeDtypeStruct((n, N_SEG * D_MODEL), F32),
        compiler_params=pltpu.CompilerParams(
            dimension_semantics=("arbitrary",), vmem_limit_bytes=VMEM_LIMIT),
        name="sample_proj",
    )(x, g_pre, w_all, bf_pad)


ROWS_M = H_M * T8


def _mem_sample_kernel(q_ref, k_ref, v_ref, o_ref):
    lane_head = lax.broadcasted_iota(jnp.int32, (T8, D_MODEL), 1) // HD_M
    q8 = q_ref[0]
    qbd = jnp.concatenate(
        [jnp.where(lane_head == hh, q8, 0.0) for hh in range(H_M)], axis=0).astype(BF16)
    s = _dot_nt(qbd, k_ref[0].astype(BF16)) * MEM_SCALE
    s = s - jnp.max(s, axis=-1, keepdims=True)
    p = jnp.exp(s)
    p = p / jnp.sum(p, axis=-1, keepdims=True)
    out = _dot(p.astype(BF16), v_ref[0].astype(BF16))
    o8 = jnp.zeros((T8, D_MODEL), F32)
    for hh in range(H_M):
        o8 = o8 + jnp.where(lane_head == hh, out[hh * T8:(hh + 1) * T8, :], 0.0)
    o_ref[0] = o8


def _mem_sample(qm8, mem_k, mem_v):
    nb = qm8.shape[0]
    per_b = lambda b: (b, 0, 0)
    return pl.pallas_call(
        _mem_sample_kernel,
        grid=(nb,),
        in_specs=[pl.BlockSpec((1, T8, D_MODEL), per_b),
                  pl.BlockSpec((1, N_MEM, D_MODEL), per_b),
                  pl.BlockSpec((1, N_MEM, D_MODEL), per_b)],
        out_specs=pl.BlockSpec((1, T8, D_MODEL), per_b),
        out_shape=jax.ShapeDtypeStruct((nb, T8, D_MODEL), F32),
        compiler_params=pltpu.CompilerParams(
            dimension_semantics=("arbitrary",), vmem_limit_bytes=VMEM_LIMIT),
        name="sample_mem_attention",
    )(qm8, mem_k, mem_v)


def _merge_sample_kernel(x_ref, p_ref, st0_ref, st1_ref, of_ref, om_ref, wconv_ref, bconv_ref,
                         wupc_ref, wupf_ref, wupm_ref, wo_ref, g_ref, y_ref, u_ref, ubuf,
                         *, n_new):
    n = x_ref.shape[0]

    def seg(k):
        return p_ref[:, k * D_MODEL:(k + 1) * D_MODEL]

    u = seg(2) * seg(0)
    u_ref[...] = u
    ubuf[0:8, :] = jnp.zeros((8, D_MODEL), F32)
    ubuf[8:8 + n, :] = u
    t = lax.broadcasted_iota(jnp.int32, (n, D_MODEL), 0) % n_new
    u1 = jnp.where(t == 0, st1_ref[...], ubuf[7:7 + n, :])
    u2 = jnp.where(t == 0, st0_ref[...], jnp.where(t == 1, st1_ref[...], ubuf[6:6 + n, :]))
    wc = wconv_ref[...]
    conv = wc[0:1, :] * u2 + wc[1:2, :] * u1 + wc[2:3, :] * u + bconv_ref[...]
    o_c = seg(1) * conv * _silu(seg(3))
    o_f = of_ref[...] * _silu(seg(7))
    o_m = om_ref[...] * _silu(seg(9))
    m = (_sigmoid(seg(10)) * _dot(o_c.astype(BF16), wupc_ref[...])
         + _sigmoid(seg(11)) * _dot(o_f.astype(BF16), wupf_ref[...])
         + _sigmoid(seg(12)) * _dot(o_m.astype(BF16), wupm_ref[...]))
    y_ref[...] = x_ref[...] + _rms(_dot(m.astype(BF16), wo_ref[...]), g_ref[...])


def _merge_sample(x, ps, st0, st1, of, om, w_conv, b_conv, wupc, wupf, wupm, wo, g_post, n_new):
    n = x.shape[0]
    return pl.pallas_call(
        functools.partial(_merge_sample_kernel, n_new=n_new),
        out_shape=(jax.ShapeDtypeStruct((n, D_MODEL), F32),
                   jax.ShapeDtypeStruct((n, D_MODEL), F32)),
        scratch_shapes=[pltpu.VMEM((n + 8, D_MODEL), F32)],
        compiler_params=pltpu.CompilerParams(vmem_limit_bytes=VMEM_LIMIT),
        name="sample_merge_out",
    )(x, ps, st0, st1, of, om, w_conv, b_conv, wupc, wupf, wupm, wo, g_post)


def kernel(x_prompt, x_sample, cache_fox_k, cache_fox_v, cache_fox_logf, state_conv,
           cache_mem_k, cache_mem_v, page_table, mem_prompt,
           g_pre, w_in, b_f, w_conv, b_conv, g_mem, w_mem_kv,
           w_up_conv, w_up_fox, w_up_mem, w_o, g_post):
    bsz, seq, _ = x_prompt.shape
    assert bsz == 1 and seq % BLK == 0 and BLK % TM == 0
    nb, n_new, _ = x_sample.shape
    assert n_new <= T8
    assert cache_fox_k.shape[1] == PAGE

    o = D_MODEL
    n_front = SEG_QM * o
    w_all = jnp.concatenate(
        [w_in[:, :n_front], w_in[:, n_front + H_F:],
         jnp.pad(w_in[:, n_front:n_front + H_F], ((0, 0), (0, o - H_F)))], axis=1).astype(BF16)
    w_v = w_all[:, SEG_V * o:(SEG_V + 1) * o]
    b_f_slot = jnp.pad(b_f, (0, SLOT - H_F)).reshape(1, SLOT)
    b_f_seg = jnp.pad(b_f, (0, o - H_F)).reshape(1, o)
    wupc, wupf, wupm, wo = (w.astype(BF16) for w in (w_up_conv, w_up_fox, w_up_mem, w_o))
    g_pre2, g_mem2, g_post2 = (g.reshape(1, o) for g in (g_pre, g_mem, g_post))
    b_conv2 = b_conv.reshape(1, o)
    eq, ek = _decay_placement()
    w_vt = jnp.pad(w_v.T.reshape(H_F, HD_F, o), ((0, 0), (0, VROWS - HD_F), (0, 0)))
    w_vt = w_vt.reshape(H_F * VROWS, o)
    v_one = jnp.asarray((np.arange(H_F * VROWS) % VROWS == HD_F).astype(np.float32)[:, None])

    xp = x_prompt.reshape(seq, o)
    mkv, mkv_b = _mem_kv(mem_prompt.reshape(N_MEM, o), g_mem2, w_mem_kv.astype(BF16))
    mcm, cstate = _branch_cm(xp, g_pre2, w_all, w_conv, b_conv2,
                             mkv_b[:, :o], mkv_b[:, o:], wupc, wupm)
    k_p, v_p, qx, kx, vt, zf, sgf, logf_p = _proj_fox(
        xp, g_pre2, w_all, w_vt, v_one, b_f_slot, eq, ek)

    xs = x_sample.reshape(nb * n_new, o)
    ps = _proj_sample(xs, g_pre2, w_all, b_f_seg)

    def seg(k):
        return ps[:, k * o:(k + 1) * o]

    def pad8(a):
        a = a.reshape(nb, n_new, a.shape[-1])
        return jnp.pad(a, ((0, 0), (0, T8 - n_new), (0, 0)))

    k_s, v_s = seg(5), seg(6)
    logf_s = ps[:, SEG_FORGET * o:SEG_FORGET * o + H_F]
    n_phys = cache_fox_k.shape[0]
    cache_kt = jnp.transpose(cache_fox_k, (0, 2, 3, 1)).reshape(n_phys, o, PAGE)
    cache_vt = jnp.transpose(cache_fox_v, (0, 2, 3, 1)).reshape(n_phys, o, PAGE)
    cache_lft = jnp.transpose(cache_fox_logf, (0, 2, 1))
    lfn_t = jnp.pad(jnp.transpose(logf_s.reshape(nb, n_new, H_F), (0, 2, 1)),
                    ((0, 0), (0, 0), (0, PAGE - n_new)))
    o_f, of8 = _attn_both(qx, kx, vt, page_table, seg(SEG_Q).reshape(nb, n_new, o),
                          pad8(k_s), pad8(v_s), lfn_t, cache_kt, cache_vt, cache_lft, n_new)
    y_p = _final_prompt(xp, mcm, o_f, zf, sgf, wupf, wo, g_post2)
    om8 = _mem_sample(pad8(seg(8)), cache_mem_k.reshape(nb, N_MEM, o),
                      cache_mem_v.reshape(nb, N_MEM, o))
    st0 = jnp.repeat(state_conv[:, 0, :], n_new, axis=0)
    st1 = jnp.repeat(state_conv[:, 1, :], n_new, axis=0)
    y_s, u_s = _merge_sample(xs, ps, st0, st1,
                             of8[:, :n_new].reshape(nb * n_new, o),
                             om8[:, :n_new].reshape(nb * n_new, o),
                             w_conv, b_conv2, wupc, wupf, wupm, wo, g_post2, n_new)

    return (y_p.reshape(1, seq, o),
            y_s.reshape(nb, n_new, o),
            k_p.reshape(1, seq, H_F, HD_F),
            v_p.reshape(1, seq, H_F, HD_F),
            logf_p.reshape(1, seq, H_F),
            cstate.reshape(1, 2, o),
            mkv[:, :o].reshape(1, N_MEM, H_M, HD_M),
            mkv[:, o:].reshape(1, N_MEM, H_M, HD_M),
            k_s.reshape(nb, n_new, H_F, HD_F),
            v_s.reshape(nb, n_new, H_F, HD_F),
            logf_s.reshape(nb, n_new, H_F),
            u_s.reshape(nb, n_new, o)[:, n_new - 2:, :])
```

```python
import functools

import numpy as np
import jax
import jax.numpy as jnp
from jax import lax
from jax.experimental import pallas as pl
from jax.experimental.pallas import tpu as pltpu

D_MODEL = 1024
H_F = 16
HD_F = 64
H_M = 4
HD_M = 256
N_MEM = 256
PAGE = 128
RMS_EPS = 1e-6
LOG2E = 1.4426950408889634
FOX_QSCALE = (HD_F ** -0.5) * LOG2E
MEM_SCALE = HD_M ** -0.5

F32 = jnp.float32
BF16 = jnp.bfloat16

TM = 256
BLK = 512
SUB = 256
LOOKAHEAD = 2
SLOT = 128
VROWS = 80

(SEG_XC, SEG_BC, SEG_CC, SEG_ZC, SEG_Q, SEG_K, SEG_V, SEG_ZF,
 SEG_QM, SEG_ZM, SEG_GA, SEG_GF, SEG_GM, SEG_FORGET) = range(14)
N_SEG = 14

VMEM_LIMIT = 56 * 1024 * 1024

_NT = (((1,), (1,)), ((), ()))


def _dot(a, b):
    return jnp.dot(a, b, preferred_element_type=F32)


def _dot_nt(a, b):
    return lax.dot_general(a, b, _NT, preferred_element_type=F32)


def _rms(x, g):
    ms = jnp.mean(x * x, axis=-1, keepdims=True)
    return x * lax.rsqrt(ms + RMS_EPS) * g


def _sigmoid(x):
    return 1.0 / (1.0 + jnp.exp(-x))


def _silu(x):
    return x * _sigmoid(x)


def _log_sigmoid(x):
    return -(jnp.maximum(-x, 0.0) + jnp.log1p(jnp.exp(-jnp.abs(x))))


def _split3(x):
    hi = x.astype(BF16)
    r = x - hi.astype(F32)
    mid = r.astype(BF16)
    lo = (r - mid.astype(F32)).astype(BF16)
    return hi, mid, lo


def _lower_tri(n):
    row = lax.broadcasted_iota(jnp.int32, (n, n), 0)
    col = lax.broadcasted_iota(jnp.int32, (n, n), 1)
    return (col <= row).astype(BF16)


def _cumsum_rows(x):
    tri = _lower_tri(x.shape[0])
    a, b, c = _split3(x)
    return _dot(tri, a) + _dot(tri, b) + _dot(tri, c)


def _const_spec(shape):
    zeros = (0,) * len(shape)
    return pl.BlockSpec(shape, lambda *_: zeros, pipeline_mode=pl.Buffered(1))


def _memkv_kernel(mem_ref, g_ref, w_ref, kv_ref, kvb_ref):
    n = _rms(mem_ref[...], g_ref[...]).astype(BF16)
    kv = _dot(n, w_ref[...])
    kv_ref[...] = kv
    kvb_ref[...] = kv.astype(BF16)


def _mem_kv(mem, g_mem, w_kv_b):
    return pl.pallas_call(
        _memkv_kernel,
        out_shape=(jax.ShapeDtypeStruct((N_MEM, 2 * D_MODEL), F32),
                   jax.ShapeDtypeStruct((N_MEM, 2 * D_MODEL), BF16)),
        compiler_params=pltpu.CompilerParams(vmem_limit_bytes=VMEM_LIMIT),
        name="mem_kv",
    )(mem, g_mem, w_kv_b)


def _mem_attention(qm, mk_ref, mv_ref):
    parts = []
    for hh in range(H_M):
        sl = slice(hh * HD_M, (hh + 1) * HD_M)
        s = _dot_nt(qm[:, sl].astype(BF16), mk_ref[:, sl]) * MEM_SCALE
        s = s - jnp.max(s, axis=-1, keepdims=True)
        p = jnp.exp(s)
        p = p / jnp.sum(p, axis=-1, keepdims=True)
        parts.append(_dot(p.astype(BF16), mv_ref[:, sl]))
    return jnp.concatenate(parts, axis=1)


def _branch_cm_kernel(x_ref, g_ref, wc4_ref, wmem_ref, wga_ref, wgm_ref, wconv_ref, bconv_ref,
                      mk_ref, mv_ref, wupc_ref, wupm_ref, mcm_ref, cstate_ref, ubuf):
    i = pl.program_id(0)

    @pl.when(i == 0)
    def _():
        ubuf[0:8, :] = jnp.zeros((8, D_MODEL), F32)

    h = _rms(x_ref[...], g_ref[...]).astype(BF16)

    def seg(w_ref, k):
        return _dot(h, w_ref[:, k * D_MODEL:(k + 1) * D_MODEL])

    u = seg(wc4_ref, 2) * seg(wc4_ref, 0)
    ubuf[8:8 + TM, :] = u
    u1 = ubuf[7:7 + TM, :]
    u2 = ubuf[6:6 + TM, :]
    wc = wconv_ref[...]
    conv = wc[0:1, :] * u2 + wc[1:2, :] * u1 + wc[2:3, :] * u + bconv_ref[...]
    cstate_ref[...] = ubuf[TM + 6:TM + 8, :]
    ubuf[0:8, :] = ubuf[TM:TM + 8, :]
    o_c = seg(wc4_ref, 1) * conv * _silu(seg(wc4_ref, 3))
    m = _sigmoid(seg(wga_ref, 0)) * _dot(o_c.astype(BF16), wupc_ref[...])

    o_m = _mem_attention(seg(wmem_ref, 0), mk_ref, mv_ref) * _silu(seg(wmem_ref, 1))
    m = m + _sigmoid(seg(wgm_ref, 0)) * _dot(o_m.astype(BF16), wupm_ref[...])
    mcm_ref[...] = m


def _wcols(n_seg, first_seg):
    return pl.BlockSpec((D_MODEL, n_seg * D_MODEL), lambda *_: (0, first_seg // n_seg),
                        pipeline_mode=pl.Buffered(1))


def _branch_cm(x, g_pre, w_all, w_conv, b_conv, mk_b, mv_b, wupc, wupm):
    s = x.shape[0]
    row = lambda i: (i, 0)
    return pl.pallas_call(
        _branch_cm_kernel,
        grid=(s // TM,),
        in_specs=[
            pl.BlockSpec((TM, D_MODEL), row),
            _const_spec((1, D_MODEL)),
            _wcols(4, SEG_XC),
            _wcols(2, SEG_QM),
            _wcols(1, SEG_GA),
            _wcols(1, SEG_GM),
            _const_spec((3, D_MODEL)),
            _const_spec((1, D_MODEL)),
            _const_spec((N_MEM, D_MODEL)),
            _const_spec((N_MEM, D_MODEL)),
            _const_spec((D_MODEL, D_MODEL)),
            _const_spec((D_MODEL, D_MODEL)),
        ],
        out_specs=(pl.BlockSpec((TM, D_MODEL), row),
                   pl.BlockSpec((2, D_MODEL), lambda i: (0, 0))),
        out_shape=(jax.ShapeDtypeStruct((s, D_MODEL), F32),
                   jax.ShapeDtypeStruct((2, D_MODEL), F32)),
        scratch_shapes=[pltpu.VMEM((TM + 8, D_MODEL), F32)],
        compiler_params=pltpu.CompilerParams(
            dimension_semantics=("arbitrary",), vmem_limit_bytes=VMEM_LIMIT),
        name="prompt_branch_cm",
    )(x, g_pre, w_all, w_all, w_all, w_all, w_conv, b_conv, mk_b, mv_b, wupc, wupm)


def _decay_placement():
    eq = np.zeros((3, SLOT, D_MODEL), np.float32)
    ek = np.zeros((3, SLOT, D_MODEL), np.float32)
    for h in range(H_F):
        base = h * HD_F
        for part in range(3):
            eq[part, h, base + part] = 1.0
            ek[part, h, base + 3 + part] = 1.0
        eq[0, H_F, base + 3:base + 6] = 1.0
        ek[0, H_F, base:base + 3] = 1.0
    return (jnp.asarray(eq.reshape(3 * SLOT, D_MODEL), BF16),
            jnp.asarray(ek.reshape(3 * SLOT, D_MODEL), BF16))


def _store_head_slots(out_ref, x, aug):
    low = lax.broadcasted_iota(jnp.int32, (x.shape[0], SLOT), 1) < HD_F
    for pair in range(H_F // 2):
        xp = x[:, pair * SLOT:(pair + 1) * SLOT]
        ap = aug[:, pair * SLOT:(pair + 1) * SLOT]
        out_ref[2 * pair] = jnp.where(low, xp, pltpu.roll(ap, HD_F, axis=1)).astype(BF16)
        out_ref[2 * pair + 1] = jnp.where(low, pltpu.roll(xp, HD_F, axis=1), ap).astype(BF16)


def _proj_fox_kernel(x_ref, g_ref, wq_ref, wk_ref, wvt_ref, vone_ref,
                     wzf_ref, wgf_ref, wf_ref, bf_ref, eq_ref, ek_ref,
                     k_ref, vtf_ref, qx_ref, kx_ref, vt_ref, zf_ref, sgf_ref,
                     logf_ref, carry):
    i = pl.program_id(0)

    @pl.when(i == 0)
    def _():
        carry[...] = jnp.zeros((1, SLOT), F32)

    h = _rms(x_ref[...], g_ref[...]).astype(BF16)
    k = _dot(h, wk_ref[...])
    k_ref[...] = k
    vt = _dot_nt(wvt_ref[...], h)
    for hh in range(H_F):
        vtf_ref[hh * HD_F:(hh + 1) * HD_F, :] = vt[hh * VROWS:hh * VROWS + HD_F, :]
    vt_ref[...] = (vt + vone_ref[...]).astype(BF16)
    zf_ref[...] = _dot(h, wzf_ref[...]).astype(BF16)
    sgf_ref[...] = _sigmoid(_dot(h, wgf_ref[...])).astype(BF16)

    lf = _log_sigmoid(_dot(h, wf_ref[...]) + bf_ref[...])
    logf_ref[...] = lf[:, 0:H_F]
    dcum = _cumsum_rows(lf) + carry[...]
    carry[...] = dcum[TM - 1:TM, :]
    dh, dm, dl = _split3(dcum * LOG2E)
    one_lane = lax.broadcasted_iota(jnp.int32, (TM, SLOT), 1) == H_F
    one = jnp.ones((TM, SLOT), BF16)

    q_aug = _dot(jnp.concatenate([jnp.where(one_lane, one, dh), dm, dl], axis=1), eq_ref[...])
    k_aug = _dot(jnp.concatenate([jnp.where(one_lane, one, -dh), -dm, -dl], axis=1), ek_ref[...])
    _store_head_slots(qx_ref, _dot(h, wq_ref[...]) * FOX_QSCALE, q_aug)
    _store_head_slots(kx_ref, k, k_aug)


def _proj_fox(x, g_pre, w_all, wvt, vone, bf, eq, ek):
    s = x.shape[0]
    row = lambda i: (i, 0)
    slot = lambda i: (0, i, 0)
    return pl.pallas_call(
        _proj_fox_kernel,
        grid=(s // TM,),
        in_specs=[
            pl.BlockSpec((TM, D_MODEL), row),
            _const_spec((1, D_MODEL)),
            _wcols(1, SEG_Q),
            _wcols(1, SEG_K),
            _const_spec((H_F * VROWS, D_MODEL)),
            _const_spec((H_F * VROWS, 1)),
            _wcols(1, SEG_ZF),
            _wcols(1, SEG_GF),
            pl.BlockSpec((D_MODEL, SLOT), lambda i: (0, SEG_FORGET * D_MODEL // SLOT),
                         pipeline_mode=pl.Buffered(1)),
            _const_spec((1, SLOT)),
            _const_spec((3 * SLOT, D_MODEL)),
            _const_spec((3 * SLOT, D_MODEL)),
        ],
        out_specs=(
            pl.BlockSpec((TM, D_MODEL), row),
            pl.BlockSpec((D_MODEL, TM), lambda i: (0, i)),
            pl.BlockSpec((H_F, TM, SLOT), slot),
            pl.BlockSpec((H_F, TM, SLOT), slot),
            pl.BlockSpec((H_F * VROWS, TM), lambda i: (0, i)),
            pl.BlockSpec((TM, D_MODEL), row),
            pl.BlockSpec((TM, D_MODEL), row),
            pl.BlockSpec((TM, H_F), row),
        ),
        out_shape=(
            jax.ShapeDtypeStruct((s, D_MODEL), F32),
            jax.ShapeDtypeStruct((D_MODEL, s), F32),
            jax.ShapeDtypeStruct((H_F, s, SLOT), BF16),
            jax.ShapeDtypeStruct((H_F, s, SLOT), BF16),
            jax.ShapeDtypeStruct((H_F * VROWS, s), BF16),
            jax.ShapeDtypeStruct((s, D_MODEL), BF16),
            jax.ShapeDtypeStruct((s, D_MODEL), BF16),
            jax.ShapeDtypeStruct((s, H_F), F32),
        ),
        scratch_shapes=[pltpu.VMEM((1, SLOT), F32)],
        compiler_params=pltpu.CompilerParams(
            dimension_semantics=("arbitrary",), vmem_limit_bytes=VMEM_LIMIT),
        name="prompt_proj_fox",
    )(x, g_pre, w_all, w_all, wvt, vone, w_all, w_all, w_all, bf, eq, ek)


T8 = 8


def _upper_tri(n):
    row = lax.broadcasted_iota(jnp.int32, (n, n), 0)
    col = lax.broadcasted_iota(jnp.int32, (n, n), 1)
    return (row <= col).astype(BF16)


def _attn_kernel(it_ref, jt_ref, pt_ref, qx_ref, kx_ref, vt_ref,
                 sq_ref, skn_ref, svn_ref, slfn_ref, *refs, n_new, pp, chunks, sample_steps):
    k_refs = refs[0:pp]
    v_refs = refs[pp:2 * pp]
    lf_refs = refs[2 * pp:3 * pp]
    o_ref, so_ref = refs[3 * pp:3 * pp + 2]
    m_sc, acc_sc, kbuf, vbuf, bias, sm_sc, sl_sc, sacc_sc, carry = refs[3 * pp + 2:]
    step = pl.program_id(0)
    i = it_ref[step]
    j = jt_ref[step]
    chunk = lax.rem(step, chunks)
    rows = n_new * H_F
    own = (lax.broadcasted_iota(jnp.int32, (H_F, D_MODEL), 1) // HD_F
           == lax.broadcasted_iota(jnp.int32, (H_F, D_MODEL), 0))

    @pl.when(j == 0)
    def _():
        m_sc[...] = jnp.full((H_F, BLK), -jnp.inf, F32)
        acc_sc[...] = jnp.zeros((H_F * VROWS, BLK), F32)

    @pl.when(step == 0)
    def _():
        sm_sc[...] = jnp.full((rows, 1), -jnp.inf, F32)
        sl_sc[...] = jnp.zeros((rows, 1), F32)
        sacc_sc[...] = jnp.zeros((rows, D_MODEL), F32)
        carry[...] = jnp.zeros((H_F, 1), F32)

    def block_diag_q():
        q = sq_ref[0] * FOX_QSCALE
        return jnp.concatenate(
            [jnp.where(own, q[t:t + 1, :], 0.0) for t in range(n_new)], axis=0).astype(BF16)

    def decay_bias(lf_t, base):
        tri = _upper_tri(PAGE)
        a, b, d = _split3(lf_t)
        dcum = _dot(a, tri) + _dot(b, tri) + _dot(d, tri) + base
        nb = -(dcum * LOG2E)
        return dcum[:, PAGE - 1:PAGE], jnp.concatenate([nb] * n_new, axis=0)

    def sample_update(s, pv_fn, fresh):
        m_old, l_old, acc_old = sm_sc[...], sl_sc[...], sacc_sc[...]
        if fresh is not None:
            m_old = jnp.where(fresh, -jnp.inf, m_old)
            l_old = jnp.where(fresh, 0.0, l_old)
            acc_old = jnp.where(fresh, 0.0, acc_old)
        m_new = jnp.maximum(m_old, jnp.max(s, axis=1, keepdims=True))
        alpha = jnp.exp2(m_old - m_new)
        p = jnp.exp2(s - m_new)
        sl_sc[...] = alpha * l_old + jnp.sum(p, axis=1, keepdims=True)
        sacc_sc[...] = alpha * acc_old + pv_fn(p.astype(BF16))
        sm_sc[...] = m_new

    def sample_logits():
        base = jnp.where(chunk == 0, 0.0, carry[...])
        for r in range(pp):
            cols = slice(r * PAGE, (r + 1) * PAGE)
            kbuf[:, cols] = k_refs[r][0].astype(BF16)
            vbuf[:, cols] = v_refs[r][0].astype(BF16)
            base, bias[:, cols] = decay_bias(lf_refs[r][0], base)
        carry[...] = base
        return _dot(block_diag_q(), kbuf[...]) + bias[...]

    def colmax(a):
        half = a.shape[0] // 2
        return jnp.maximum(jnp.max(a[:half], axis=0, keepdims=True),
                           jnp.max(a[half:], axis=0, keepdims=True))

    def logits(sub, hh):
        return _dot_nt(kx_ref[hh, sub * SUB:(sub + 1) * SUB, :], qx_ref[hh])

    def block(masked):
        units = [(sub, hh) for sub in range(BLK // SUB) for hh in range(H_F)]
        qry = lax.broadcasted_iota(jnp.int32, (SUB, BLK), 1)
        key = lax.broadcasted_iota(jnp.int32, (SUB, BLK), 0)
        s_smp = sample_logits()
        pending = [logits(*u) for u in units[:LOOKAHEAD]]
        for n, (sub, hh) in enumerate(units):
            if n == len(units) // 2:
                sample_update(s_smp, lambda p: _dot_nt(p, vbuf[...]), chunk == 0)
            if n + LOOKAHEAD < len(units):
                pending.append(logits(*units[n + LOOKAHEAD]))
            st = pending.pop(0)
            if masked:
                st = jnp.where(key + sub * SUB <= qry, st, -jnp.inf)
            m_old = m_sc[hh:hh + 1, :]
            m_new = jnp.maximum(m_old, colmax(st))
            alpha = jnp.exp2(m_old - m_new)
            p = jnp.exp2(st - m_new)
            m_sc[hh:hh + 1, :] = m_new
            rows = slice(hh * VROWS, (hh + 1) * VROWS)
            pv = _dot(vt_ref[rows, sub * SUB:(sub + 1) * SUB], p.astype(BF16))
            acc_sc[rows, :] = alpha * acc_sc[rows, :] + pv

    @pl.when(j < i)
    def _():
        block(False)

    @pl.when(j == i)
    def _():
        block(True)
        heads = []
        for hh in range(H_F):
            base = hh * VROWS
            inv = 1.0 / acc_sc[base + HD_F:base + HD_F + 1, :]
            heads.append(acc_sc[base:base + HD_F, :] * inv)
        o_ref[...] = jnp.concatenate(heads, axis=0).T.astype(BF16)

    @pl.when((chunk == chunks - 1) & (step < sample_steps))
    def _():
        pad = jnp.zeros((PAGE - T8, D_MODEL), F32)
        kn = jnp.concatenate([skn_ref[0], pad], axis=0).astype(BF16)
        vn = jnp.concatenate([svn_ref[0], pad], axis=0).astype(BF16)
        _, b_new = decay_bias(slfn_ref[0], carry[...])
        s = _dot_nt(block_diag_q(), kn) + b_new
        tok = lax.broadcasted_iota(jnp.int32, (rows, PAGE), 1)
        t_row = lax.broadcasted_iota(jnp.int32, (rows, PAGE), 0) // H_F
        s = jnp.where((tok < n_new) & (tok <= t_row), s, -jnp.inf)
        sample_update(s, lambda p: _dot(p, vn), None)
        out = sacc_sc[...] * (1.0 / sl_sc[...])
        so_ref[0] = jnp.zeros((T8, D_MODEL), F32)
        for t in range(n_new):
            piece = jnp.where(own, out[t * H_F:(t + 1) * H_F, :], 0.0)
            so_ref[0, t:t + 1, :] = jnp.sum(piece, axis=0, keepdims=True)


def _attn_both(qx, kx, vt, page_table, q, kn8, vn8, lfn_t, cache_kt, cache_vt, cache_lft, n_new):
    s = qx.shape[1]
    nblk = s // BLK
    it = np.concatenate([np.full(i + 1, i) for i in range(nblk)]).astype(np.int32)
    jt = np.concatenate([np.arange(i + 1) for i in range(nblk)]).astype(np.int32)
    n_steps = it.shape[0]
    nb, n_pages = page_table.shape
    pp = min(d for d in range(1, n_pages + 1)
             if n_pages % d == 0 and nb * (n_pages // d) <= n_steps)
    chunks = n_pages // pp
    sample_steps = nb * chunks
    rows = n_new * H_F

    def per_seq(t, it, jt, pt):
        return (jnp.minimum(t // chunks, nb - 1), 0, 0)

    pt_steps = page_table.reshape(sample_steps, pp)
    pt_steps = jnp.concatenate(
        [pt_steps, jnp.broadcast_to(pt_steps[-1:], (n_steps - sample_steps, pp))], axis=0)

    def page_map(r):
        return lambda t, it, jt, pt: (pt[t * pp + r], 0, 0)

    in_specs = [
        pl.BlockSpec((H_F, BLK, SLOT), lambda t, it, jt, pt: (0, it[t], 0)),
        pl.BlockSpec((H_F, BLK, SLOT), lambda t, it, jt, pt: (0, jt[t], 0)),
        pl.BlockSpec((H_F * VROWS, BLK), lambda t, it, jt, pt: (0, jt[t])),
        pl.BlockSpec((1, n_new, D_MODEL), per_seq),
        pl.BlockSpec((1, T8, D_MODEL), per_seq),
        pl.BlockSpec((1, T8, D_MODEL), per_seq),
        pl.BlockSpec((1, H_F, PAGE), per_seq),
    ]
    in_specs += [pl.BlockSpec((1, D_MODEL, PAGE), page_map(r)) for r in range(pp)]
    in_specs += [pl.BlockSpec((1, D_MODEL, PAGE), page_map(r)) for r in range(pp)]
    in_specs += [pl.BlockSpec((1, H_F, PAGE), page_map(r)) for r in range(pp)]
    grid_spec = pltpu.PrefetchScalarGridSpec(
        num_scalar_prefetch=3,
        grid=(n_steps,),
        in_specs=in_specs,
        out_specs=(pl.BlockSpec((BLK, D_MODEL), lambda t, it, jt, pt: (it[t], 0)),
                   pl.BlockSpec((1, T8, D_MODEL), per_seq)),
        scratch_shapes=[
            pltpu.VMEM((H_F, BLK), F32),
            pltpu.VMEM((H_F * VROWS, BLK), F32),
            pltpu.VMEM((D_MODEL, pp * PAGE), BF16),
            pltpu.VMEM((D_MODEL, pp * PAGE), BF16),
            pltpu.VMEM((rows, pp * PAGE), F32),
            pltpu.VMEM((rows, 1), F32),
            pltpu.VMEM((rows, 1), F32),
            pltpu.VMEM((rows, D_MODEL), F32),
            pltpu.VMEM((H_F, 1), F32),
        ],
    )
    args = [jnp.asarray(it), jnp.asarray(jt), pt_steps.reshape(-1), qx, kx, vt, q, kn8, vn8, lfn_t]
    args += [cache_kt] * pp + [cache_vt] * pp + [cache_lft] * pp
    return pl.pallas_call(
        functools.partial(_attn_kernel, n_new=n_new, pp=pp, chunks=chunks,
                          sample_steps=sample_steps),
        grid_spec=grid_spec,
        out_shape=(jax.ShapeDtypeStruct((s, D_MODEL), BF16),
                   jax.ShapeDtypeStruct((nb, T8, D_MODEL), F32)),
        compiler_params=pltpu.CompilerParams(
            dimension_semantics=("arbitrary",), vmem_limit_bytes=VMEM_LIMIT),
        name="fox_attention",
    )(*args)


def _final_kernel(x_ref, mcm_ref, of_ref, zf_ref, sgf_ref, wupf_ref, wo_ref, g_ref, y_ref):
    og = (of_ref[...].astype(F32) * _silu(zf_ref[...].astype(F32))).astype(BF16)
    m = mcm_ref[...] + sgf_ref[...].astype(F32) * _dot(og, wupf_ref[...])
    y_ref[...] = x_ref[...] + _rms(_dot(m.astype(BF16), wo_ref[...]), g_ref[...])


def _final_prompt(x, mcm, of, zf, sgf, wupf, wo, g_post):
    s = x.shape[0]
    row = lambda i: (i, 0)
    tile = pl.BlockSpec((TM, D_MODEL), row)
    return pl.pallas_call(
        _final_kernel,
        grid=(s // TM,),
        in_specs=[tile, tile, tile, tile, tile,
                  _const_spec((D_MODEL, D_MODEL)),
                  _const_spec((D_MODEL, D_MODEL)),
                  _const_spec((1, D_MODEL))],
        out_specs=tile,
        out_shape=jax.ShapeDtypeStruct((s, D_MODEL), F32),
        compiler_params=pltpu.CompilerParams(
            dimension_semantics=("arbitrary",), vmem_limit_bytes=VMEM_LIMIT),
        name="prompt_merge_out",
    )(x, mcm, of, zf, sgf, wupf, wo, g_post)


def _proj_sample_kernel(x_ref, g_ref, w_ref, bf_ref, p_ref):
    j = pl.program_id(0)
    h = _rms(x_ref[...], g_ref[...]).astype(BF16)
    acc = _dot(h, w_ref[...])

    @pl.when(j != SEG_FORGET)
    def _():
        p_ref[...] = acc

    @pl.when(j == SEG_FORGET)
    def _():
        p_ref[...] = _log_sigmoid(acc + bf_ref[...])


def _proj_sample(x, g_pre, w_all, bf_pad):
    n = x.shape[0]
    return pl.pallas_call(
        _proj_sample_kernel,
        grid=(N_SEG,),
        in_specs=[pl.BlockSpec((n, D_MODEL), lambda j: (0, 0)),
                  pl.BlockSpec((1, D_MODEL), lambda j: (0, 0)),
                  pl.BlockSpec((D_MODEL, D_MODEL), lambda j: (0, j)),
                  pl.BlockSpec((1, D_MODEL), lambda j: (0, 0))],
        out_specs=pl.BlockSpec((n, D_MODEL), lambda j: (0, j)),
        out_shape=jax.ShapeDtypeStruct((n, N_SEG * D_MODEL), F32),
        compiler_params=pltpu.CompilerParams(
            dimension_semantics=("arbitrary",), vmem_limit_bytes=VMEM_LIMIT),
        name="sample_proj",
    )(x, g_pre, w_all, bf_pad)


ROWS_M = H_M * T8


def _mem_sample_kernel(q_ref, k_ref, v_ref, o_ref):
    lane_head = lax.broadcasted_iota(jnp.int32, (T8, D_MODEL), 1) // HD_M
    q8 = q_ref[0]
    qbd = jnp.concatenate(
        [jnp.where(lane_head == hh, q8, 0.0) for hh in range(H_M)], axis=0).astype(BF16)
    s = _dot_nt(qbd, k_ref[0].astype(BF16)) * MEM_SCALE
    s = s - jnp.max(s, axis=-1, keepdims=True)
    p = jnp.exp(s)
    p = p / jnp.sum(p, axis=-1, keepdims=True)
    out = _dot(p.astype(BF16), v_ref[0].astype(BF16))
    o8 = jnp.zeros((T8, D_MODEL), F32)
    for hh in range(H_M):
        o8 = o8 + jnp.where(lane_head == hh, out[hh * T8:(hh + 1) * T8, :], 0.0)
    o_ref[0] = o8


def _mem_sample(qm8, mem_k, mem_v):
    nb = qm8.shape[0]
    per_b = lambda b: (b, 0, 0)
    return pl.pallas_call(
        _mem_sample_kernel,
        grid=(nb,),
        in_specs=[pl.BlockSpec((1, T8, D_MODEL), per_b),
                  pl.BlockSpec((1, N_MEM, D_MODEL), per_b),
                  pl.BlockSpec((1, N_MEM, D_MODEL), per_b)],
        out_specs=pl.BlockSpec((1, T8, D_MODEL), per_b),
        out_shape=jax.ShapeDtypeStruct((nb, T8, D_MODEL), F32),
        compiler_params=pltpu.CompilerParams(
            dimension_semantics=("arbitrary",), vmem_limit_bytes=VMEM_LIMIT),
        name="sample_mem_attention",
    )(qm8, mem_k, mem_v)


def _merge_sample_kernel(x_ref, p_ref, st0_ref, st1_ref, of_ref, om_ref, wconv_ref, bconv_ref,
                         wupc_ref, wupf_ref, wupm_ref, wo_ref, g_ref, y_ref, u_ref, ubuf,
                         *, n_new):
    n = x_ref.shape[0]

    def seg(k):
        return p_ref[:, k * D_MODEL:(k + 1) * D_MODEL]

    u = seg(2) * seg(0)
    u_ref[...] = u
    ubuf[0:8, :] = jnp.zeros((8, D_MODEL), F32)
    ubuf[8:8 + n, :] = u
    t = lax.broadcasted_iota(jnp.int32, (n, D_MODEL), 0) % n_new
    u1 = jnp.where(t == 0, st1_ref[...], ubuf[7:7 + n, :])
    u2 = jnp.where(t == 0, st0_ref[...], jnp.where(t == 1, st1_ref[...], ubuf[6:6 + n, :]))
    wc = wconv_ref[...]
    conv = wc[0:1, :] * u2 + wc[1:2, :] * u1 + wc[2:3, :] * u + bconv_ref[...]
    o_c = seg(1) * conv * _silu(seg(3))
    o_f = of_ref[...] * _silu(seg(7))
    o_m = om_ref[...] * _silu(seg(9))
    m = (_sigmoid(seg(10)) * _dot(o_c.astype(BF16), wupc_ref[...])
         + _sigmoid(seg(11)) * _dot(o_f.astype(BF16), wupf_ref[...])
         + _sigmoid(seg(12)) * _dot(o_m.astype(BF16), wupm_ref[...]))
    y_ref[...] = x_ref[...] + _rms(_dot(m.astype(BF16), wo_ref[...]), g_ref[...])


def _merge_sample(x, ps, st0, st1, of, om, w_conv, b_conv, wupc, wupf, wupm, wo, g_post, n_new):
    n = x.shape[0]
    return pl.pallas_call(
        functools.partial(_merge_sample_kernel, n_new=n_new),
        out_shape=(jax.ShapeDtypeStruct((n, D_MODEL), F32),
                   jax.ShapeDtypeStruct((n, D_MODEL), F32)),
        scratch_shapes=[pltpu.VMEM((n + 8, D_MODEL), F32)],
        compiler_params=pltpu.CompilerParams(vmem_limit_bytes=VMEM_LIMIT),
        name="sample_merge_out",
    )(x, ps, st0, st1, of, om, w_conv, b_conv, wupc, wupf, wupm, wo, g_post)


def kernel(x_prompt, x_sample, cache_fox_k, cache_fox_v, cache_fox_logf, state_conv,
           cache_mem_k, cache_mem_v, page_table, mem_prompt,
           g_pre, w_in, b_f, w_conv, b_conv, g_mem, w_mem_kv,
           w_up_conv, w_up_fox, w_up_mem, w_o, g_post):
    bsz, seq, _ = x_prompt.shape
    assert bsz == 1 and seq % BLK == 0 and BLK % TM == 0
    nb, n_new, _ = x_sample.shape
    assert n_new <= T8
    assert cache_fox_k.shape[1] == PAGE

    o = D_MODEL
    n_front = SEG_QM * o
    w_all = jnp.concatenate(
        [w_in[:, :n_front], w_in[:, n_front + H_F:],
         jnp.pad(w_in[:, n_front:n_front + H_F], ((0, 0), (0, o - H_F)))], axis=1).astype(BF16)
    w_v = w_all[:, SEG_V * o:(SEG_V + 1) * o]
    b_f_slot = jnp.pad(b_f, (0, SLOT - H_F)).reshape(1, SLOT)
    b_f_seg = jnp.pad(b_f, (0, o - H_F)).reshape(1, o)
    wupc, wupf, wupm, wo = (w.astype(BF16) for w in (w_up_conv, w_up_fox, w_up_mem, w_o))
    g_pre2, g_mem2, g_post2 = (g.reshape(1, o) for g in (g_pre, g_mem, g_post))
    b_conv2 = b_conv.reshape(1, o)
    eq, ek = _decay_placement()
    w_vt = jnp.pad(w_v.T.reshape(H_F, HD_F, o), ((0, 0), (0, VROWS - HD_F), (0, 0)))
    w_vt = w_vt.reshape(H_F * VROWS, o)
    v_one = jnp.asarray((np.arange(H_F * VROWS) % VROWS == HD_F).astype(np.float32)[:, None])

    xp = x_prompt.reshape(seq, o)
    mkv, mkv_b = _mem_kv(mem_prompt.reshape(N_MEM, o), g_mem2, w_mem_kv.astype(BF16))
    mcm, cstate = _branch_cm(xp, g_pre2, w_all, w_conv, b_conv2,
                             mkv_b[:, :o], mkv_b[:, o:], wupc, wupm)
    k_p, vt_p, qx, kx, vt, zf, sgf, logf_p = _proj_fox(
        xp, g_pre2, w_all, w_vt, v_one, b_f_slot, eq, ek)

    xs = x_sample.reshape(nb * n_new, o)
    ps = _proj_sample(xs, g_pre2, w_all, b_f_seg)

    def seg(k):
        return ps[:, k * o:(k + 1) * o]

    def pad8(a):
        a = a.reshape(nb, n_new, a.shape[-1])
        return jnp.pad(a, ((0, 0), (0, T8 - n_new), (0, 0)))

    k_s, v_s = seg(5), seg(6)
    logf_s = ps[:, SEG_FORGET * o:SEG_FORGET * o + H_F]
    n_phys = cache_fox_k.shape[0]
    cache_kt = jnp.transpose(cache_fox_k, (0, 2, 3, 1)).reshape(n_phys, o, PAGE)
    cache_vt = jnp.transpose(cache_fox_v, (0, 2, 3, 1)).reshape(n_phys, o, PAGE)
    cache_lft = jnp.transpose(cache_fox_logf, (0, 2, 1))
    lfn_t = jnp.pad(jnp.transpose(logf_s.reshape(nb, n_new, H_F), (0, 2, 1)),
                    ((0, 0), (0, 0), (0, PAGE - n_new)))
    o_f, of8 = _attn_both(qx, kx, vt, page_table, seg(SEG_Q).reshape(nb, n_new, o),
                          pad8(k_s), pad8(v_s), lfn_t, cache_kt, cache_vt, cache_lft, n_new)
    y_p = _final_prompt(xp, mcm, o_f, zf, sgf, wupf, wo, g_post2)
    om8 = _mem_sample(pad8(seg(SEG_QM)), cache_mem_k.reshape(nb, N_MEM, o),
                      cache_mem_v.reshape(nb, N_MEM, o))
    st0 = jnp.repeat(state_conv[:, 0, :], n_new, axis=0)
    st1 = jnp.repeat(state_conv[:, 1, :], n_new, axis=0)
    y_s, u_s = _merge_sample(xs, ps, st0, st1,
                             of8[:, :n_new].reshape(nb * n_new, o),
                             om8[:, :n_new].reshape(nb * n_new, o),
                             w_conv, b_conv2, wupc, wupf, wupm, wo, g_post2, n_new)

    return (y_p.reshape(1, seq, o),
            y_s.reshape(nb, n_new, o),
            k_p.reshape(1, seq, H_F, HD_F),
            jnp.transpose(vt_p.reshape(1, H_F, HD_F, seq), (0, 3, 1, 2)),
            logf_p.reshape(1, seq, H_F),
            cstate.reshape(1, 2, o),
            mkv[:, :o].reshape(1, N_MEM, H_M, HD_M),
            mkv[:, o:].reshape(1, N_MEM, H_M, HD_M),
            k_s.reshape(nb, n_new, H_F, HD_F),
            v_s.reshape(nb, n_new, H_F, HD_F),
            logf_s.reshape(nb, n_new, H_F),
            u_s.reshape(nb, n_new, o)[:, n_new - 2:, :])
```

```python
import functools

import numpy as np
import jax
import jax.numpy as jnp
from jax import lax
from jax.experimental import pallas as pl
from jax.experimental.pallas import tpu as pltpu

D_MODEL = 1024
H_F = 16
HD_F = 64
H_M = 4
HD_M = 256
N_MEM = 256
PAGE = 128
RMS_EPS = 1e-6
LOG2E = 1.4426950408889634
FOX_QSCALE = (HD_F ** -0.5) * LOG2E
MEM_SCALE = HD_M ** -0.5

F32 = jnp.float32
BF16 = jnp.bfloat16

TM = 256
BLK = 512
SUB = 256
LOOKAHEAD = 2
DECODE_LOGITS_AT = 0
DECODE_UPDATE_AT = 31
SLOT = 128
VROWS = 80

(SEG_XC, SEG_BC, SEG_CC, SEG_ZC, SEG_Q, SEG_K, SEG_V, SEG_ZF,
 SEG_QM, SEG_ZM, SEG_GA, SEG_GF, SEG_GM, SEG_FORGET) = range(14)
N_SEG = 14

VMEM_LIMIT = 56 * 1024 * 1024

_NT = (((1,), (1,)), ((), ()))


def _dot(a, b):
    return jnp.dot(a, b, preferred_element_type=F32)


def _dot_nt(a, b):
    return lax.dot_general(a, b, _NT, preferred_element_type=F32)


def _rms(x, g):
    ms = jnp.mean(x * x, axis=-1, keepdims=True)
    return x * lax.rsqrt(ms + RMS_EPS) * g


def _sigmoid(x):
    return 1.0 / (1.0 + jnp.exp(-x))


def _silu(x):
    return x * _sigmoid(x)


def _log_sigmoid(x):
    return -(jnp.maximum(-x, 0.0) + jnp.log1p(jnp.exp(-jnp.abs(x))))


def _split3(x):
    hi = x.astype(BF16)
    r = x - hi.astype(F32)
    mid = r.astype(BF16)
    lo = (r - mid.astype(F32)).astype(BF16)
    return hi, mid, lo


def _lower_tri(n):
    row = lax.broadcasted_iota(jnp.int32, (n, n), 0)
    col = lax.broadcasted_iota(jnp.int32, (n, n), 1)
    return (col <= row).astype(BF16)


def _cumsum_rows(x):
    tri = _lower_tri(x.shape[0])
    a, b, c = _split3(x)
    return _dot(tri, a) + _dot(tri, b) + _dot(tri, c)


def _const_spec(shape):
    zeros = (0,) * len(shape)
    return pl.BlockSpec(shape, lambda *_: zeros, pipeline_mode=pl.Buffered(1))


def _memkv_kernel(mem_ref, g_ref, w_ref, kv_ref, kvb_ref):
    n = _rms(mem_ref[...], g_ref[...]).astype(BF16)
    kv = _dot(n, w_ref[...])
    kv_ref[...] = kv
    kvb_ref[...] = kv.astype(BF16)


def _mem_kv(mem, g_mem, w_kv_b):
    return pl.pallas_call(
        _memkv_kernel,
        out_shape=(jax.ShapeDtypeStruct((N_MEM, 2 * D_MODEL), F32),
                   jax.ShapeDtypeStruct((N_MEM, 2 * D_MODEL), BF16)),
        compiler_params=pltpu.CompilerParams(vmem_limit_bytes=VMEM_LIMIT),
        name="mem_kv",
    )(mem, g_mem, w_kv_b)


def _mem_attention(qm, mk_ref, mv_ref):
    parts = []
    for hh in range(H_M):
        sl = slice(hh * HD_M, (hh + 1) * HD_M)
        s = _dot_nt(qm[:, sl].astype(BF16), mk_ref[:, sl]) * MEM_SCALE
        s = s - jnp.max(s, axis=-1, keepdims=True)
        p = jnp.exp(s)
        p = p / jnp.sum(p, axis=-1, keepdims=True)
        parts.append(_dot(p.astype(BF16), mv_ref[:, sl]))
    return jnp.concatenate(parts, axis=1)


def _branch_cm_kernel(x_ref, g_ref, wc4_ref, wmem_ref, wga_ref, wgm_ref, wconv_ref, bconv_ref,
                      mk_ref, mv_ref, wupc_ref, wupm_ref, mcm_ref, cstate_ref, ubuf):
    i = pl.program_id(0)

    @pl.when(i == 0)
    def _():
        ubuf[0:8, :] = jnp.zeros((8, D_MODEL), F32)

    h = _rms(x_ref[...], g_ref[...]).astype(BF16)

    def seg(w_ref, k):
        return _dot(h, w_ref[:, k * D_MODEL:(k + 1) * D_MODEL])

    u = seg(wc4_ref, 2) * seg(wc4_ref, 0)
    ubuf[8:8 + TM, :] = u
    u1 = ubuf[7:7 + TM, :]
    u2 = ubuf[6:6 + TM, :]
    wc = wconv_ref[...]
    conv = wc[0:1, :] * u2 + wc[1:2, :] * u1 + wc[2:3, :] * u + bconv_ref[...]
    cstate_ref[...] = ubuf[TM + 6:TM + 8, :]
    ubuf[0:8, :] = ubuf[TM:TM + 8, :]
    o_c = seg(wc4_ref, 1) * conv * _silu(seg(wc4_ref, 3))
    m = _sigmoid(seg(wga_ref, 0)) * _dot(o_c.astype(BF16), wupc_ref[...])

    o_m = _mem_attention(seg(wmem_ref, 0), mk_ref, mv_ref) * _silu(seg(wmem_ref, 1))
    m = m + _sigmoid(seg(wgm_ref, 0)) * _dot(o_m.astype(BF16), wupm_ref[...])
    mcm_ref[...] = m


def _wcols(n_seg, first_seg):
    return pl.BlockSpec((D_MODEL, n_seg * D_MODEL), lambda *_: (0, first_seg // n_seg),
                        pipeline_mode=pl.Buffered(1))


def _branch_cm(x, g_pre, w_all, w_conv, b_conv, mk_b, mv_b, wupc, wupm):
    s = x.shape[0]
    row = lambda i: (i, 0)
    return pl.pallas_call(
        _branch_cm_kernel,
        grid=(s // TM,),
        in_specs=[
            pl.BlockSpec((TM, D_MODEL), row),
            _const_spec((1, D_MODEL)),
            _wcols(4, SEG_XC),
            _wcols(2, SEG_QM),
            _wcols(1, SEG_GA),
            _wcols(1, SEG_GM),
            _const_spec((3, D_MODEL)),
            _const_spec((1, D_MODEL)),
            _const_spec((N_MEM, D_MODEL)),
            _const_spec((N_MEM, D_MODEL)),
            _const_spec((D_MODEL, D_MODEL)),
            _const_spec((D_MODEL, D_MODEL)),
        ],
        out_specs=(pl.BlockSpec((TM, D_MODEL), row),
                   pl.BlockSpec((2, D_MODEL), lambda i: (0, 0))),
        out_shape=(jax.ShapeDtypeStruct((s, D_MODEL), F32),
                   jax.ShapeDtypeStruct((2, D_MODEL), F32)),
        scratch_shapes=[pltpu.VMEM((TM + 8, D_MODEL), F32)],
        compiler_params=pltpu.CompilerParams(
            dimension_semantics=("arbitrary",), vmem_limit_bytes=VMEM_LIMIT),
        name="prompt_branch_cm",
    )(x, g_pre, w_all, w_all, w_all, w_all, w_conv, b_conv, mk_b, mv_b, wupc, wupm)


def _decay_placement():
    eq = np.zeros((3, SLOT, D_MODEL), np.float32)
    ek = np.zeros((3, SLOT, D_MODEL), np.float32)
    for h in range(H_F):
        base = h * HD_F
        for part in range(3):
            eq[part, h, base + part] = 1.0
            ek[part, h, base + 3 + part] = 1.0
        eq[0, H_F, base + 3:base + 6] = 1.0
        ek[0, H_F, base:base + 3] = 1.0
    return (jnp.asarray(eq.reshape(3 * SLOT, D_MODEL), BF16),
            jnp.asarray(ek.reshape(3 * SLOT, D_MODEL), BF16))


def _store_head_slots(out_ref, x, aug):
    low = lax.broadcasted_iota(jnp.int32, (x.shape[0], SLOT), 1) < HD_F
    for pair in range(H_F // 2):
        xp = x[:, pair * SLOT:(pair + 1) * SLOT]
        ap = aug[:, pair * SLOT:(pair + 1) * SLOT]
        out_ref[2 * pair] = jnp.where(low, xp, pltpu.roll(ap, HD_F, axis=1)).astype(BF16)
        out_ref[2 * pair + 1] = jnp.where(low, pltpu.roll(xp, HD_F, axis=1), ap).astype(BF16)


def _proj_fox_kernel(x_ref, g_ref, wq_ref, wk_ref, wvt_ref, vone_ref,
                     wzf_ref, wgf_ref, wf_ref, bf_ref, eq_ref, ek_ref,
                     k_ref, vtf_ref, qx_ref, kx_ref, vt_ref, zf_ref, sgf_ref,
                     logf_ref, carry):
    i = pl.program_id(0)

    @pl.when(i == 0)
    def _():
        carry[...] = jnp.zeros((1, SLOT), F32)

    h = _rms(x_ref[...], g_ref[...]).astype(BF16)
    k = _dot(h, wk_ref[...])
    k_ref[...] = k
    vt = _dot_nt(wvt_ref[...], h)
    for hh in range(H_F):
        vtf_ref[hh * HD_F:(hh + 1) * HD_F, :] = vt[hh * VROWS:hh * VROWS + HD_F, :]
    vt_ref[...] = (vt + vone_ref[...]).astype(BF16)
    zf_ref[...] = _dot(h, wzf_ref[...]).astype(BF16)
    sgf_ref[...] = _sigmoid(_dot(h, wgf_ref[...])).astype(BF16)

    lf = _log_sigmoid(_dot(h, wf_ref[...]) + bf_ref[...])
    logf_ref[...] = lf[:, 0:H_F]
    dcum = _cumsum_rows(lf) + carry[...]
    carry[...] = dcum[TM - 1:TM, :]
    dh, dm, dl = _split3(dcum * LOG2E)
    one_lane = lax.broadcasted_iota(jnp.int32, (TM, SLOT), 1) == H_F
    one = jnp.ones((TM, SLOT), BF16)

    q_aug = _dot(jnp.concatenate([jnp.where(one_lane, one, dh), dm, dl], axis=1), eq_ref[...])
    k_aug = _dot(jnp.concatenate([jnp.where(one_lane, one, -dh), -dm, -dl], axis=1), ek_ref[...])
    _store_head_slots(qx_ref, _dot(h, wq_ref[...]) * FOX_QSCALE, q_aug)
    _store_head_slots(kx_ref, k, k_aug)


def _proj_fox(x, g_pre, w_all, wvt, vone, bf, eq, ek):
    s = x.shape[0]
    row = lambda i: (i, 0)
    slot = lambda i: (0, i, 0)
    return pl.pallas_call(
        _proj_fox_kernel,
        grid=(s // TM,),
        in_specs=[
            pl.BlockSpec((TM, D_MODEL), row),
            _const_spec((1, D_MODEL)),
            _wcols(1, SEG_Q),
            _wcols(1, SEG_K),
            _const_spec((H_F * VROWS, D_MODEL)),
            _const_spec((H_F * VROWS, 1)),
            _wcols(1, SEG_ZF),
            _wcols(1, SEG_GF),
            pl.BlockSpec((D_MODEL, SLOT), lambda i: (0, SEG_FORGET * D_MODEL // SLOT),
                         pipeline_mode=pl.Buffered(1)),
            _const_spec((1, SLOT)),
            _const_spec((3 * SLOT, D_MODEL)),
            _const_spec((3 * SLOT, D_MODEL)),
        ],
        out_specs=(
            pl.BlockSpec((TM, D_MODEL), row),
            pl.BlockSpec((D_MODEL, TM), lambda i: (0, i)),
            pl.BlockSpec((H_F, TM, SLOT), slot),
            pl.BlockSpec((H_F, TM, SLOT), slot),
            pl.BlockSpec((H_F * VROWS, TM), lambda i: (0, i)),
            pl.BlockSpec((TM, D_MODEL), row),
            pl.BlockSpec((TM, D_MODEL), row),
            pl.BlockSpec((TM, H_F), row),
        ),
        out_shape=(
            jax.ShapeDtypeStruct((s, D_MODEL), F32),
            jax.ShapeDtypeStruct((D_MODEL, s), F32),
            jax.ShapeDtypeStruct((H_F, s, SLOT), BF16),
            jax.ShapeDtypeStruct((H_F, s, SLOT), BF16),
            jax.ShapeDtypeStruct((H_F * VROWS, s), BF16),
            jax.ShapeDtypeStruct((s, D_MODEL), BF16),
            jax.ShapeDtypeStruct((s, D_MODEL), BF16),
            jax.ShapeDtypeStruct((s, H_F), F32),
        ),
        scratch_shapes=[pltpu.VMEM((1, SLOT), F32)],
        compiler_params=pltpu.CompilerParams(
            dimension_semantics=("arbitrary",), vmem_limit_bytes=VMEM_LIMIT),
        name="prompt_proj_fox",
    )(x, g_pre, w_all, w_all, wvt, vone, w_all, w_all, w_all, bf, eq, ek)


T8 = 8


def _upper_tri(n):
    row = lax.broadcasted_iota(jnp.int32, (n, n), 0)
    col = lax.broadcasted_iota(jnp.int32, (n, n), 1)
    return (row <= col).astype(BF16)


def _attn_kernel(it_ref, jt_ref, pt_ref, qx_ref, kx_ref, vt_ref,
                 sq_ref, skn_ref, svn_ref, slfn_ref, *refs, n_new, pp, chunks, sample_steps):
    k_refs = refs[0:pp]
    v_refs = refs[pp:2 * pp]
    lf_refs = refs[2 * pp:3 * pp]
    o_ref, so_ref = refs[3 * pp:3 * pp + 2]
    m_sc, acc_sc, kbuf, vbuf, bias, sm_sc, sl_sc, sacc_sc, carry = refs[3 * pp + 2:]
    step = pl.program_id(0)
    i = it_ref[step]
    j = jt_ref[step]
    chunk = lax.rem(step, chunks)
    rows = n_new * H_F
    own = (lax.broadcasted_iota(jnp.int32, (H_F, D_MODEL), 1) // HD_F
           == lax.broadcasted_iota(jnp.int32, (H_F, D_MODEL), 0))

    @pl.when(j == 0)
    def _():
        m_sc[...] = jnp.full((H_F, BLK), -jnp.inf, F32)
        acc_sc[...] = jnp.zeros((H_F * VROWS, BLK), F32)

    @pl.when(step == 0)
    def _():
        sm_sc[...] = jnp.full((rows, 1), -jnp.inf, F32)
        sl_sc[...] = jnp.zeros((rows, 1), F32)
        sacc_sc[...] = jnp.zeros((rows, D_MODEL), F32)
        carry[...] = jnp.zeros((H_F, 1), F32)

    def block_diag_q():
        q = sq_ref[0] * FOX_QSCALE
        return jnp.concatenate(
            [jnp.where(own, q[t:t + 1, :], 0.0) for t in range(n_new)], axis=0).astype(BF16)

    def decay_bias(lf_t, base):
        tri = _upper_tri(PAGE)
        a, b, d = _split3(lf_t)
        dcum = _dot(a, tri) + _dot(b, tri) + _dot(d, tri) + base
        nb = -(dcum * LOG2E)
        return dcum[:, PAGE - 1:PAGE], jnp.concatenate([nb] * n_new, axis=0)

    def sample_update(s, pv_fn, fresh):
        m_old, l_old, acc_old = sm_sc[...], sl_sc[...], sacc_sc[...]
        if fresh is not None:
            m_old = jnp.where(fresh, -jnp.inf, m_old)
            l_old = jnp.where(fresh, 0.0, l_old)
            acc_old = jnp.where(fresh, 0.0, acc_old)
        m_new = jnp.maximum(m_old, jnp.max(s, axis=1, keepdims=True))
        alpha = jnp.exp2(m_old - m_new)
        p = jnp.exp2(s - m_new)
        sl_sc[...] = alpha * l_old + jnp.sum(p, axis=1, keepdims=True)
        sacc_sc[...] = alpha * acc_old + pv_fn(p.astype(BF16))
        sm_sc[...] = m_new

    def sample_logits():
        base = jnp.where(chunk == 0, 0.0, carry[...])
        for r in range(pp):
            cols = slice(r * PAGE, (r + 1) * PAGE)
            kbuf[:, cols] = k_refs[r][0].astype(BF16)
            vbuf[:, cols] = v_refs[r][0].astype(BF16)
            base, bias[:, cols] = decay_bias(lf_refs[r][0], base)
        carry[...] = base
        return _dot(block_diag_q(), kbuf[...]) + bias[...]

    def colmax(a):
        half = a.shape[0] // 2
        return jnp.maximum(jnp.max(a[:half], axis=0, keepdims=True),
                           jnp.max(a[half:], axis=0, keepdims=True))

    def logits(sub, hh):
        return _dot_nt(kx_ref[hh, sub * SUB:(sub + 1) * SUB, :], qx_ref[hh])

    def block(masked):
        units = [(sub, hh) for sub in range(BLK // SUB) for hh in range(H_F)]
        qry = lax.broadcasted_iota(jnp.int32, (SUB, BLK), 1)
        key = lax.broadcasted_iota(jnp.int32, (SUB, BLK), 0)
        pending = [logits(*u) for u in units[:LOOKAHEAD]]
        for n, (sub, hh) in enumerate(units):
            if n == DECODE_LOGITS_AT:
                s_smp = sample_logits()
            if n == DECODE_UPDATE_AT:
                sample_update(s_smp, lambda p: _dot_nt(p, vbuf[...]), chunk == 0)
            if n + LOOKAHEAD < len(units):
                pending.append(logits(*units[n + LOOKAHEAD]))
            st = pending.pop(0)
            if masked:
                st = jnp.where(key + sub * SUB <= qry, st, -jnp.inf)
            m_old = m_sc[hh:hh + 1, :]
            m_new = jnp.maximum(m_old, colmax(st))
            alpha = jnp.exp2(m_old - m_new)
            p = jnp.exp2(st - m_new)
            m_sc[hh:hh + 1, :] = m_new
            rows = slice(hh * VROWS, (hh + 1) * VROWS)
            pv = _dot(vt_ref[rows, sub * SUB:(sub + 1) * SUB], p.astype(BF16))
            acc_sc[rows, :] = alpha * acc_sc[rows, :] + pv

    @pl.when(j < i)
    def _():
        block(False)

    @pl.when(j == i)
    def _():
        block(True)
        heads = []
        for hh in range(H_F):
            base = hh * VROWS
            inv = 1.0 / acc_sc[base + HD_F:base + HD_F + 1, :]
            heads.append(acc_sc[base:base + HD_F, :] * inv)
        o_ref[...] = jnp.concatenate(heads, axis=0).T.astype(BF16)

    @pl.when((chunk == chunks - 1) & (step < sample_steps))
    def _():
        pad = jnp.zeros((PAGE - T8, D_MODEL), F32)
        kn = jnp.concatenate([skn_ref[0], pad], axis=0).astype(BF16)
        vn = jnp.concatenate([svn_ref[0], pad], axis=0).astype(BF16)
        _, b_new = decay_bias(slfn_ref[0], carry[...])
        s = _dot_nt(block_diag_q(), kn) + b_new
        tok = lax.broadcasted_iota(jnp.int32, (rows, PAGE), 1)
        t_row = lax.broadcasted_iota(jnp.int32, (rows, PAGE), 0) // H_F
        s = jnp.where((tok < n_new) & (tok <= t_row), s, -jnp.inf)
        sample_update(s, lambda p: _dot(p, vn), None)
        out = sacc_sc[...] * (1.0 / sl_sc[...])
        so_ref[0] = jnp.zeros((T8, D_MODEL), F32)
        for t in range(n_new):
            piece = jnp.where(own, out[t * H_F:(t + 1) * H_F, :], 0.0)
            so_ref[0, t:t + 1, :] = jnp.sum(piece, axis=0, keepdims=True)


def _attn_both(qx, kx, vt, page_table, q, kn8, vn8, lfn_t, cache_kt, cache_vt, cache_lft, n_new):
    s = qx.shape[1]
    nblk = s // BLK
    it = np.concatenate([np.full(i + 1, i) for i in range(nblk)]).astype(np.int32)
    jt = np.concatenate([np.arange(i + 1) for i in range(nblk)]).astype(np.int32)
    n_steps = it.shape[0]
    nb, n_pages = page_table.shape
    pp = min(d for d in range(1, n_pages + 1)
             if n_pages % d == 0 and nb * (n_pages // d) <= n_steps)
    chunks = n_pages // pp
    sample_steps = nb * chunks
    rows = n_new * H_F

    def per_seq(t, it, jt, pt):
        return (jnp.minimum(t // chunks, nb - 1), 0, 0)

    pt_steps = page_table.reshape(sample_steps, pp)
    pt_steps = jnp.concatenate(
        [pt_steps, jnp.broadcast_to(pt_steps[-1:], (n_steps - sample_steps, pp))], axis=0)

    def page_map(r):
        return lambda t, it, jt, pt: (pt[t * pp + r], 0, 0)

    in_specs = [
        pl.BlockSpec((H_F, BLK, SLOT), lambda t, it, jt, pt: (0, it[t], 0)),
        pl.BlockSpec((H_F, BLK, SLOT), lambda t, it, jt, pt: (0, jt[t], 0)),
        pl.BlockSpec((H_F * VROWS, BLK), lambda t, it, jt, pt: (0, jt[t])),
        pl.BlockSpec((1, n_new, D_MODEL), per_seq),
        pl.BlockSpec((1, T8, D_MODEL), per_seq),
        pl.BlockSpec((1, T8, D_MODEL), per_seq),
        pl.BlockSpec((1, H_F, PAGE), per_seq),
    ]
    in_specs += [pl.BlockSpec((1, D_MODEL, PAGE), page_map(r)) for r in range(pp)]
    in_specs += [pl.BlockSpec((1, D_MODEL, PAGE), page_map(r)) for r in range(pp)]
    in_specs += [pl.BlockSpec((1, H_F, PAGE), page_map(r)) for r in range(pp)]
    grid_spec = pltpu.PrefetchScalarGridSpec(
        num_scalar_prefetch=3,
        grid=(n_steps,),
        in_specs=in_specs,
        out_specs=(pl.BlockSpec((BLK, D_MODEL), lambda t, it, jt, pt: (it[t], 0)),
                   pl.BlockSpec((1, T8, D_MODEL), per_seq)),
        scratch_shapes=[
            pltpu.VMEM((H_F, BLK), F32),
            pltpu.VMEM((H_F * VROWS, BLK), F32),
            pltpu.VMEM((D_MODEL, pp * PAGE), BF16),
            pltpu.VMEM((D_MODEL, pp * PAGE), BF16),
            pltpu.VMEM((rows, pp * PAGE), F32),
            pltpu.VMEM((rows, 1), F32),
            pltpu.VMEM((rows, 1), F32),
            pltpu.VMEM((rows, D_MODEL), F32),
            pltpu.VMEM((H_F, 1), F32),
        ],
    )
    args = [jnp.asarray(it), jnp.asarray(jt), pt_steps.reshape(-1), qx, kx, vt, q, kn8, vn8, lfn_t]
    args += [cache_kt] * pp + [cache_vt] * pp + [cache_lft] * pp
    return pl.pallas_call(
        functools.partial(_attn_kernel, n_new=n_new, pp=pp, chunks=chunks,
                          sample_steps=sample_steps),
        grid_spec=grid_spec,
        out_shape=(jax.ShapeDtypeStruct((s, D_MODEL), BF16),
                   jax.ShapeDtypeStruct((nb, T8, D_MODEL), F32)),
        compiler_params=pltpu.CompilerParams(
            dimension_semantics=("arbitrary",), vmem_limit_bytes=VMEM_LIMIT),
        name="fox_attention",
    )(*args)


def _final_kernel(x_ref, mcm_ref, of_ref, zf_ref, sgf_ref, wupf_ref, wo_ref, g_ref, y_ref):
    og = (of_ref[...].astype(F32) * _silu(zf_ref[...].astype(F32))).astype(BF16)
    m = mcm_ref[...] + sgf_ref[...].astype(F32) * _dot(og, wupf_ref[...])
    y_ref[...] = x_ref[...] + _rms(_dot(m.astype(BF16), wo_ref[...]), g_ref[...])


def _final_prompt(x, mcm, of, zf, sgf, wupf, wo, g_post):
    s = x.shape[0]
    row = lambda i: (i, 0)
    tile = pl.BlockSpec((TM, D_MODEL), row)
    return pl.pallas_call(
        _final_kernel,
        grid=(s // TM,),
        in_specs=[tile, tile, tile, tile, tile,
                  _const_spec((D_MODEL, D_MODEL)),
                  _const_spec((D_MODEL, D_MODEL)),
                  _const_spec((1, D_MODEL))],
        out_specs=tile,
        out_shape=jax.ShapeDtypeStruct((s, D_MODEL), F32),
        compiler_params=pltpu.CompilerParams(
            dimension_semantics=("arbitrary",), vmem_limit_bytes=VMEM_LIMIT),
        name="prompt_merge_out",
    )(x, mcm, of, zf, sgf, wupf, wo, g_post)


def _proj_sample_kernel(x_ref, g_ref, w_ref, bf_ref, p_ref):
    j = pl.program_id(0)
    h = _rms(x_ref[...], g_ref[...]).astype(BF16)
    acc = _dot(h, w_ref[...])

    @pl.when(j != SEG_FORGET)
    def _():
        p_ref[...] = acc

    @pl.when(j == SEG_FORGET)
    def _():
        p_ref[...] = _log_sigmoid(acc + bf_ref[...])


def _proj_sample(x, g_pre, w_all, bf_pad):
    n = x.shape[0]
    return pl.pallas_call(
        _proj_sample_kernel,
        grid=(N_SEG,),
        in_specs=[pl.BlockSpec((n, D_MODEL), lambda j: (0, 0)),
                  pl.BlockSpec((1, D_MODEL), lambda j: (0, 0)),
                  pl.BlockSpec((D_MODEL, D_MODEL), lambda j: (0, j)),
                  pl.BlockSpec((1, D_MODEL), lambda j: (0, 0))],
        out_specs=pl.BlockSpec((n, D_MODEL), lambda j: (0, j)),
        out_shape=jax.ShapeDtypeStruct((n, N_SEG * D_MODEL), F32),
        compiler_params=pltpu.CompilerParams(
            dimension_semantics=("arbitrary",), vmem_limit_bytes=VMEM_LIMIT),
        name="sample_proj",
    )(x, g_pre, w_all, bf_pad)


ROWS_M = H_M * T8


def _mem_sample_kernel(q_ref, k_ref, v_ref, o_ref):
    lane_head = lax.broadcasted_iota(jnp.int32, (T8, D_MODEL), 1) // HD_M
    q8 = q_ref[0]
    qbd = jnp.concatenate(
        [jnp.where(lane_head == hh, q8, 0.0) for hh in range(H_M)], axis=0).astype(BF16)
    s = _dot_nt(qbd, k_ref[0].astype(BF16)) * MEM_SCALE
    s = s - jnp.max(s, axis=-1, keepdims=True)
    p = jnp.exp(s)
    p = p / jnp.sum(p, axis=-1, keepdims=True)
    out = _dot(p.astype(BF16), v_ref[0].astype(BF16))
    o8 = jnp.zeros((T8, D_MODEL), F32)
    for hh in range(H_M):
        o8 = o8 + jnp.where(lane_head == hh, out[hh * T8:(hh + 1) * T8, :], 0.0)
    o_ref[0] = o8


def _mem_sample(qm8, mem_k, mem_v):
    nb = qm8.shape[0]
    per_b = lambda b: (b, 0, 0)
    return pl.pallas_call(
        _mem_sample_kernel,
        grid=(nb,),
        in_specs=[pl.BlockSpec((1, T8, D_MODEL), per_b),
                  pl.BlockSpec((1, N_MEM, D_MODEL), per_b),
                  pl.BlockSpec((1, N_MEM, D_MODEL), per_b)],
        out_specs=pl.BlockSpec((1, T8, D_MODEL), per_b),
        out_shape=jax.ShapeDtypeStruct((nb, T8, D_MODEL), F32),
        compiler_params=pltpu.CompilerParams(
            dimension_semantics=("arbitrary",), vmem_limit_bytes=VMEM_LIMIT),
        name="sample_mem_attention",
    )(qm8, mem_k, mem_v)


def _merge_sample_kernel(x_ref, p_ref, st0_ref, st1_ref, of_ref, om_ref, wconv_ref, bconv_ref,
                         wupc_ref, wupf_ref, wupm_ref, wo_ref, g_ref, y_ref, u_ref, ubuf,
                         *, n_new):
    n = x_ref.shape[0]

    def seg(k):
        return p_ref[:, k * D_MODEL:(k + 1) * D_MODEL]

    u = seg(2) * seg(0)
    u_ref[...] = u
    ubuf[0:8, :] = jnp.zeros((8, D_MODEL), F32)
    ubuf[8:8 + n, :] = u
    t = lax.broadcasted_iota(jnp.int32, (n, D_MODEL), 0) % n_new
    u1 = jnp.where(t == 0, st1_ref[...], ubuf[7:7 + n, :])
    u2 = jnp.where(t == 0, st0_ref[...], jnp.where(t == 1, st1_ref[...], ubuf[6:6 + n, :]))
    wc = wconv_ref[...]
    conv = wc[0:1, :] * u2 + wc[1:2, :] * u1 + wc[2:3, :] * u + bconv_ref[...]
    o_c = seg(1) * conv * _silu(seg(3))
    o_f = of_ref[...] * _silu(seg(7))
    o_m = om_ref[...] * _silu(seg(9))
    m = (_sigmoid(seg(10)) * _dot(o_c.astype(BF16), wupc_ref[...])
         + _sigmoid(seg(11)) * _dot(o_f.astype(BF16), wupf_ref[...])
         + _sigmoid(seg(12)) * _dot(o_m.astype(BF16), wupm_ref[...]))
    y_ref[...] = x_ref[...] + _rms(_dot(m.astype(BF16), wo_ref[...]), g_ref[...])


def _merge_sample(x, ps, st0, st1, of, om, w_conv, b_conv, wupc, wupf, wupm, wo, g_post, n_new):
    n = x.shape[0]
    return pl.pallas_call(
        functools.partial(_merge_sample_kernel, n_new=n_new),
        out_shape=(jax.ShapeDtypeStruct((n, D_MODEL), F32),
                   jax.ShapeDtypeStruct((n, D_MODEL), F32)),
        scratch_shapes=[pltpu.VMEM((n + 8, D_MODEL), F32)],
        compiler_params=pltpu.CompilerParams(vmem_limit_bytes=VMEM_LIMIT),
        name="sample_merge_out",
    )(x, ps, st0, st1, of, om, w_conv, b_conv, wupc, wupf, wupm, wo, g_post)


def kernel(x_prompt, x_sample, cache_fox_k, cache_fox_v, cache_fox_logf, state_conv,
           cache_mem_k, cache_mem_v, page_table, mem_prompt,
           g_pre, w_in, b_f, w_conv, b_conv, g_mem, w_mem_kv,
           w_up_conv, w_up_fox, w_up_mem, w_o, g_post):
    bsz, seq, _ = x_prompt.shape
    assert bsz == 1 and seq % BLK == 0 and BLK % TM == 0
    nb, n_new, _ = x_sample.shape
    assert n_new <= T8
    assert cache_fox_k.shape[1] == PAGE

    o = D_MODEL
    n_front = SEG_QM * o
    w_all = jnp.concatenate(
        [w_in[:, :n_front], w_in[:, n_front + H_F:],
         jnp.pad(w_in[:, n_front:n_front + H_F], ((0, 0), (0, o - H_F)))], axis=1).astype(BF16)
    w_v = w_all[:, SEG_V * o:(SEG_V + 1) * o]
    b_f_slot = jnp.pad(b_f, (0, SLOT - H_F)).reshape(1, SLOT)
    b_f_seg = jnp.pad(b_f, (0, o - H_F)).reshape(1, o)
    wupc, wupf, wupm, wo = (w.astype(BF16) for w in (w_up_conv, w_up_fox, w_up_mem, w_o))
    g_pre2, g_mem2, g_post2 = (g.reshape(1, o) for g in (g_pre, g_mem, g_post))
    b_conv2 = b_conv.reshape(1, o)
    eq, ek = _decay_placement()
    w_vt = jnp.pad(w_v.T.reshape(H_F, HD_F, o), ((0, 0), (0, VROWS - HD_F), (0, 0)))
    w_vt = w_vt.reshape(H_F * VROWS, o)
    v_one = jnp.asarray((np.arange(H_F * VROWS) % VROWS == HD_F).astype(np.float32)[:, None])

    xp = x_prompt.reshape(seq, o)
    mkv, mkv_b = _mem_kv(mem_prompt.reshape(N_MEM, o), g_mem2, w_mem_kv.astype(BF16))
    mcm, cstate = _branch_cm(xp, g_pre2, w_all, w_conv, b_conv2,
                             mkv_b[:, :o], mkv_b[:, o:], wupc, wupm)
    k_p, vt_p, qx, kx, vt, zf, sgf, logf_p = _proj_fox(
        xp, g_pre2, w_all, w_vt, v_one, b_f_slot, eq, ek)

    xs = x_sample.reshape(nb * n_new, o)
    ps = _proj_sample(xs, g_pre2, w_all, b_f_seg)

    def seg(k):
        return ps[:, k * o:(k + 1) * o]

    def pad8(a):
        a = a.reshape(nb, n_new, a.shape[-1])
        return jnp.pad(a, ((0, 0), (0, T8 - n_new), (0, 0)))

    k_s, v_s = seg(5), seg(6)
    logf_s = ps[:, SEG_FORGET * o:SEG_FORGET * o + H_F]
    n_phys = cache_fox_k.shape[0]
    cache_kt = jnp.transpose(cache_fox_k, (0, 2, 3, 1)).reshape(n_phys, o, PAGE)
    cache_vt = jnp.transpose(cache_fox_v, (0, 2, 3, 1)).reshape(n_phys, o, PAGE)
    cache_lft = jnp.transpose(cache_fox_logf, (0, 2, 1))
    lfn_t = jnp.pad(jnp.transpose(logf_s.reshape(nb, n_new, H_F), (0, 2, 1)),
                    ((0, 0), (0, 0), (0, PAGE - n_new)))
    o_f, of8 = _attn_both(qx, kx, vt, page_table, seg(SEG_Q).reshape(nb, n_new, o),
                          pad8(k_s), pad8(v_s), lfn_t, cache_kt, cache_vt, cache_lft, n_new)
    y_p = _final_prompt(xp, mcm, o_f, zf, sgf, wupf, wo, g_post2)
    om8 = _mem_sample(pad8(seg(SEG_QM)), cache_mem_k.reshape(nb, N_MEM, o),
                      cache_mem_v.reshape(nb, N_MEM, o))
    st0 = jnp.repeat(state_conv[:, 0, :], n_new, axis=0)
    st1 = jnp.repeat(state_conv[:, 1, :], n_new, axis=0)
    y_s, u_s = _merge_sample(xs, ps, st0, st1,
                             of8[:, :n_new].reshape(nb * n_new, o),
                             om8[:, :n_new].reshape(nb * n_new, o),
                             w_conv, b_conv2, wupc, wupf, wupm, wo, g_post2, n_new)

    return (y_p.reshape(1, seq, o),
            y_s.reshape(nb, n_new, o),
            k_p.reshape(1, seq, H_F, HD_F),
            jnp.transpose(vt_p.reshape(1, H_F, HD_F, seq), (0, 3, 1, 2)),
            logf_p.reshape(1, seq, H_F),
            cstate.reshape(1, 2, o),
            mkv[:, :o].reshape(1, N_MEM, H_M, HD_M),
            mkv[:, o:].reshape(1, N_MEM, H_M, HD_M),
            k_s.reshape(nb, n_new, H_F, HD_F),
            v_s.reshape(nb, n_new, H_F, HD_F),
            logf_s.reshape(nb, n_new, H_F),
            u_s.reshape(nb, n_new, o)[:, n_new - 2:, :])
```
